```python
import math
import jax, jax.numpy as jnp
from jax import lax
import numpy as np

D_MODEL = 1024
BATCH = 4
SEQ = 8192
DEPTH = 1

HEAD_DIM = 64
DSA_HEADS = 8
IDX_HEADS = 8
IDX_DIM = 64
TOPK_MAX = 256
SWA_HEADS = 8
SWA_KV_HEADS = 2
WINDOW = 128
BLOCK = 128
MEM_TOKENS = 256
MEM_HEADS = 4
MEM_HEAD_DIM = D_MODEL // MEM_HEADS
D_MIX = (DSA_HEADS + SWA_HEADS) * HEAD_DIM
D_FF = 4 * D_MODEL
ROPE_THETA = 10000.0
LN_EPS = 1e-5
DEEPNORM_ALPHA = (2.0 * DEPTH) ** 0.25
DEEPNORM_BETA = (8.0 * DEPTH) ** -0.25
IN_SIZES = (DSA_HEADS * HEAD_DIM, HEAD_DIM, HEAD_DIM,
            IDX_HEADS * IDX_DIM, IDX_DIM, IDX_HEADS,
            SWA_HEADS * HEAD_DIM, SWA_KV_HEADS * HEAD_DIM, SWA_KV_HEADS * HEAD_DIM)
IN_COLS = sum(IN_SIZES)

kernel_name = "hymba_dsa_swa_sink_deepnorm_layer"


def layer_norm(x, g, b):
    xf = x.astype(jnp.float32)
    mu = jnp.mean(xf, axis=-1, keepdims=True)
    var = jnp.mean(jnp.square(xf - mu), axis=-1, keepdims=True)
    y = (xf - mu) * lax.rsqrt(var + LN_EPS)
    return (y * g.astype(jnp.float32) + b.astype(jnp.float32)).astype(x.dtype)


def rope_tables(positions, dim):
    inv = ROPE_THETA ** (-jnp.arange(0, dim, 2, dtype=jnp.float32) / dim)
    ang = positions.astype(jnp.float32)[..., None] * inv
    return jnp.cos(ang), jnp.sin(ang)


def apply_rope(x, cos, sin):
    half = x.shape[-1] // 2
    xf = x.astype(jnp.float32)
    x1, x2 = xf[..., :half], xf[..., half:]
    c, s = cos[:, :, None, :], sin[:, :, None, :]
    return jnp.concatenate([x1 * c - x2 * s, x2 * c + x1 * s], axis=-1).astype(x.dtype)


def dsa_sparse_attention(q, k, v, iq, ik, iw):
    B, S, H, dh = q.shape
    n_sel = min(TOPK_MAX, S // 4)
    nb = S // BLOCK
    key_pos = jnp.arange(S)
    scale = HEAD_DIM ** -0.5
    idx_scale = (IDX_DIM ** -0.5) * (IDX_HEADS ** -0.5)

    def to_blocks(a):
        return jnp.moveaxis(a.reshape((B, nb, BLOCK) + a.shape[2:]), 1, 0)

    def one_block(args):
        qb, iqb, iwb, start = args
        qpos = start + jnp.arange(BLOCK)
        causal = key_pos[None, :] <= qpos[:, None]
        rel = jax.nn.relu(jnp.einsum('bthd,bsd->bths', iqb, ik).astype(jnp.float32))
        score = jnp.einsum('bths,bth->bts', rel, iwb.astype(jnp.float32)) * idx_scale
        score = jnp.where(causal[None], score, -jnp.inf)
        _, sel = lax.top_k(score, n_sel)
        valid = sel <= qpos[None, :, None]
        k_sel = jax.vmap(lambda kb, ib: kb[ib])(k, sel)
        v_sel = jax.vmap(lambda vb, ib: vb[ib])(v, sel)
        logits = jnp.einsum('bthd,btnd->bthn', qb, k_sel).astype(jnp.float32) * scale
        logits = jnp.where(valid[:, :, None, :], logits, -jnp.inf)
        p = jax.nn.softmax(logits, axis=-1).astype(v.dtype)
        return jnp.einsum('bthn,btnd->bthd', p, v_sel)

    starts = jnp.arange(nb) * BLOCK
    out = lax.map(one_block, (to_blocks(q), to_blocks(iq), to_blocks(iw), starts))
    return jnp.moveaxis(out, 0, 1).reshape(B, S, H * dh)


def swa_sink_attention(q, k, v, sinks):
    B, S, Hq, dh = q.shape
    Hkv = k.shape[2]
    G = Hq // Hkv
    nb = S // BLOCK
    scale = dh ** -0.5
    qb = q.reshape(B, nb, BLOCK, Hkv, G, dh)

    def band(a):
        ab = a.reshape(B, nb, BLOCK, Hkv, dh)
        prev = jnp.pad(ab, ((0, 0), (1, 0), (0, 0), (0, 0), (0, 0)))[:, :-1]
        return jnp.concatenate([prev, ab], axis=2)

    kb, vb = band(k), band(v)
    qpos = jnp.arange(BLOCK)[:, None] + BLOCK
    kpos = jnp.arange(2 * BLOCK)[None, :]
    diff = qpos - kpos
    inband = (diff >= 0) & (diff < WINDOW)
    first = (jnp.arange(nb)[:, None, None] == 0) & (kpos[None] < BLOCK)
    valid = inband[None] & jnp.logical_not(first)
    logits = jnp.einsum('bnqhgd,bnkhd->bnhgqk', qb, kb).astype(jnp.float32) * scale
    logits = jnp.where(valid[None, :, None, None], logits, -jnp.inf)
    sink = jnp.broadcast_to(sinks.astype(jnp.float32).reshape(1, 1, Hkv, G, 1, 1),
                            logits.shape[:-1] + (1,))
    p = jax.nn.softmax(jnp.concatenate([logits, sink], axis=-1), axis=-1)[..., :-1]
    out = jnp.einsum('bnhgqk,bnkhd->bnqhgd', p.astype(v.dtype), vb)
    return out.reshape(B, S, Hq * dh)


def memory_cross_attention(x, mem, wq, wk, wv, wo):
    B, S, _ = x.shape
    M = mem.shape[1]
    q = (x @ wq).reshape(B, S, MEM_HEADS, MEM_HEAD_DIM)
    k = (mem @ wk).reshape(B, M, MEM_HEADS, MEM_HEAD_DIM)
    v = (mem @ wv).reshape(B, M, MEM_HEADS, MEM_HEAD_DIM)
    logits = jnp.einsum('bshd,bmhd->bhsm', q, k).astype(jnp.float32) * (MEM_HEAD_DIM ** -0.5)
    p = jax.nn.softmax(logits, axis=-1).astype(v.dtype)
    o = jnp.einsum('bhsm,bmhd->bshd', p, v).reshape(B, S, MEM_HEADS * MEM_HEAD_DIM)
    return o @ wo


def squared_relu_mlp(x, w_up, w_down):
    return jnp.square(jax.nn.relu(x @ w_up)) @ w_down


def setup_inputs(seed: int = 0) -> dict:
    key = jax.random.key(seed)
    ks = jax.random.split(key, 20)
    f32 = jnp.float32
    n = lambda k, shape, s: jax.random.normal(k, shape, f32) * s
    x = n(ks[0], (BATCH, SEQ, D_MODEL), 1.0)
    mem = n(ks[1], (BATCH, MEM_TOKENS, D_MODEL), 1.0)
    positions = jnp.broadcast_to(jnp.arange(SEQ, dtype=jnp.int32)[None, :], (BATCH, SEQ))
    return {
        "x": x,
        "mem": mem,
        "positions": positions,
        "w_in": n(ks[2], (DEPTH, D_MODEL, IN_COLS), D_MODEL ** -0.5),
        "b_in": n(ks[3], (DEPTH, IN_COLS), 0.01),
        "swa_sinks": n(ks[4], (DEPTH, SWA_HEADS), 0.5),
        "w_o": n(ks[5], (DEPTH, D_MIX, D_MODEL), D_MIX ** -0.5 * DEEPNORM_BETA),
        "ln1_g": 1.0 + n(ks[6], (DEPTH, D_MODEL), 0.02),
        "ln1_b": n(ks[7], (DEPTH, D_MODEL), 0.02),
        "wq_mem": n(ks[8], (DEPTH, D_MODEL, D_MODEL), D_MODEL ** -0.5),
        "wk_mem": n(ks[9], (DEPTH, D_MODEL, D_MODEL), D_MODEL ** -0.5),
        "wv_mem": n(ks[10], (DEPTH, D_MODEL, D_MODEL), D_MODEL ** -0.5),
        "wo_mem": n(ks[11], (DEPTH, D_MODEL, D_MODEL), D_MODEL ** -0.5 * DEEPNORM_BETA),
        "ln2_g": 1.0 + n(ks[12], (DEPTH, D_MODEL), 0.02),
        "ln2_b": n(ks[13], (DEPTH, D_MODEL), 0.02),
        "w_up": n(ks[14], (DEPTH, D_MODEL, D_FF), D_MODEL ** -0.5 * DEEPNORM_BETA),
        "w_down": n(ks[15], (DEPTH, D_FF, D_MODEL), D_FF ** -0.5 * DEEPNORM_BETA),
        "ln3_g": 1.0 + n(ks[16], (DEPTH, D_MODEL), 0.02),
        "ln3_b": n(ks[17], (DEPTH, D_MODEL), 0.02),
    }


def reference(x, mem, positions, w_in, b_in, swa_sinks, w_o, ln1_g, ln1_b,
              wq_mem, wk_mem, wv_mem, wo_mem, ln2_g, ln2_b,
              w_up, w_down, ln3_g, ln3_b):
    B, S, _ = x.shape
    cos, sin = rope_tables(positions, HEAD_DIM)
    offsets = np.cumsum(IN_SIZES)[:-1].tolist()
    h = x
    for l in range(DEPTH):
        proj = h @ w_in[l] + b_in[l]
        dq, dk, dv, iq, ik, iw, sq, sk, sv = jnp.split(proj, offsets, axis=-1)
        dq = apply_rope(dq.reshape(B, S, DSA_HEADS, HEAD_DIM), cos, sin)
        dk = apply_rope(dk.reshape(B, S, 1, HEAD_DIM), cos, sin)[:, :, 0]
        iq = apply_rope(iq.reshape(B, S, IDX_HEADS, IDX_DIM), cos, sin)
        ik = apply_rope(ik.reshape(B, S, 1, IDX_DIM), cos, sin)[:, :, 0]
        o_dsa = dsa_sparse_attention(dq, dk, dv, iq, ik, iw)
        sq = apply_rope(sq.reshape(B, S, SWA_HEADS, HEAD_DIM), cos, sin)
        sk = apply_rope(sk.reshape(B, S, SWA_KV_HEADS, HEAD_DIM), cos, sin)
        sv = sv.reshape(B, S, SWA_KV_HEADS, HEAD_DIM)
        o_swa = swa_sink_attention(sq, sk, sv, swa_sinks[l])
        mix = jnp.concatenate([o_dsa, o_swa], axis=-1) @ w_o[l]
        h = layer_norm(DEEPNORM_ALPHA * h + mix, ln1_g[l], ln1_b[l])
        c = memory_cross_attention(h, mem, wq_mem[l], wk_mem[l], wv_mem[l], wo_mem[l])
        h = layer_norm(DEEPNORM_ALPHA * h + c, ln2_g[l], ln2_b[l])
        m = squared_relu_mlp(h, w_up[l], w_down[l])
        h = layer_norm(DEEPNORM_ALPHA * h + m, ln3_g[l], ln3_b[l])
    return h
```

```python
import functools

import jax
import jax.numpy as jnp
import numpy as np
from jax import lax
from jax.experimental import pallas as pl
from jax.experimental.pallas import tpu as pltpu

HEAD_DIM = 64
DSA_HEADS = 8
IDX_HEADS = 8
IDX_DIM = 64
TOPK_MAX = 256
SWA_HEADS = 8
SWA_KV_HEADS = 2
WINDOW = 128
MEM_HEADS = 4
ROPE_THETA = 10000.0
LN_EPS = 1e-5
DEPTH = 1
DEEPNORM_ALPHA = (2.0 * DEPTH) ** 0.25

LANES = 128
VMEM_LIMIT = 56 * 1024 * 1024
INT_MIN = -(2 ** 31)
KEY_NEG_INF = (0xFF800000 ^ 0x7FFFFFFF) - (1 << 32)
MASK_VALUE = -1e30

F32 = jnp.float32
BF16 = jnp.bfloat16
NT_DIMS = (((1,), (1,)), ((), ()))


def _cparams(n_axes):
    return pltpu.CompilerParams(
        dimension_semantics=("arbitrary",) * n_axes, vmem_limit_bytes=VMEM_LIMIT)


def _layer_norm(y, g, b):
    mu = jnp.mean(y, axis=-1, keepdims=True)
    d = y - mu
    var = jnp.mean(d * d, axis=-1, keepdims=True)
    return d * lax.rsqrt(var + LN_EPS) * g + b


def _half_mask(shape, low):
    lane = lax.broadcasted_iota(jnp.int32, shape, len(shape) - 1) % LANES
    return (lane < HEAD_DIM) if low else (lane >= HEAD_DIM)


def _move_half(v, src_high, dst_high):
    if src_high != dst_high:
        v = pltpu.roll(v, HEAD_DIM, 1)
    return jnp.where(_half_mask(v.shape, low=not dst_high), v, 0.0)


def _rope_table_kernel(pos_ref, inv_ref, sgn_ref, cos_ref, sin_ref):
    ang = pos_ref[...].astype(F32) * inv_ref[...]
    cos_ref[...] = jnp.cos(ang)
    sin_ref[...] = jnp.sin(ang) * sgn_ref[...]


def _rope_tables(pos_col, inv_row, sgn_row, tm):
    t = pos_col.shape[0]
    return pl.pallas_call(
        _rope_table_kernel,
        out_shape=(jax.ShapeDtypeStruct((t, LANES), F32), jax.ShapeDtypeStruct((t, LANES), F32)),
        grid=(t // tm,),
        in_specs=[pl.BlockSpec((tm, 1), lambda i: (i, 0)),
                  pl.BlockSpec((1, LANES), lambda i: (0, 0)),
                  pl.BlockSpec((1, LANES), lambda i: (0, 0))],
        out_specs=(pl.BlockSpec((tm, LANES), lambda i: (i, 0)),
                   pl.BlockSpec((tm, LANES), lambda i: (i, 0))),
        compiler_params=_cparams(1),
        name="rope_tables",
    )(pos_col, inv_row, sgn_row)


C_DQ, C_IQ, C_SQ, C_SK, C_KK, C_SV, C_MISC, C_END = 0, 512, 1024, 1536, 1664, 1792, 1920, 2048


def _inproj_kernel(x_ref, w_ref, b_ref, cos_ref, sin_ref,
                   qd_ref, qi_ref, qs_ref, sk_ref, kk_ref, sv_ref, misc_ref):
    x = x_ref[...].astype(BF16)
    cos = cos_ref[...]
    sin = sin_ref[...]
    first = (lax.broadcasted_iota(jnp.int32, cos.shape, 1) % HEAD_DIM) < (HEAD_DIM // 2)

    def proj(c0, c1):
        return jnp.dot(x, w_ref[:, c0:c1], preferred_element_type=F32) + b_ref[:, c0:c1]

    def rope(v):
        sw = jnp.where(first, pltpu.roll(v, LANES - HEAD_DIM // 2, 1), pltpu.roll(v, HEAD_DIM // 2, 1))
        return v * cos + sw * sin

    def rope_store(out_ref, c0, c1, scale):
        y = proj(c0, c1)
        for g in range((c1 - c0) // LANES):
            r = rope(y[:, g * LANES:(g + 1) * LANES])
            if scale != 1.0:
                r = r * scale
            out_ref[:, g * LANES:(g + 1) * LANES] = r.astype(out_ref.dtype)

    scale = HEAD_DIM ** -0.5
    rope_store(qd_ref, C_DQ, C_IQ, scale)
    rope_store(qi_ref, C_IQ, C_SQ, 1.0)
    rope_store(qs_ref, C_SQ, C_SK, scale)
    rope_store(sk_ref, C_SK, C_KK, 1.0)
    rope_store(kk_ref, C_KK, C_SV, 1.0)
    sv_ref[...] = proj(C_SV, C_MISC).astype(sv_ref.dtype)
    misc_ref[...] = proj(C_MISC, C_END)


def _in_proj(x2, w, b, cos, sin, tm):
    t, d = x2.shape
    row = lambda i: (i, 0)
    const = lambda i: (0, 0)
    outs = [(512, BF16), (512, BF16), (512, BF16), (128, BF16), (128, BF16), (128, BF16), (128, F32)]
    return pl.pallas_call(
        _inproj_kernel,
        out_shape=tuple(jax.ShapeDtypeStruct((t, n), dt) for n, dt in outs),
        grid=(t // tm,),
        in_specs=[pl.BlockSpec((tm, d), row),
                  pl.BlockSpec((d, C_END), const),
                  pl.BlockSpec((1, C_END), const),
                  pl.BlockSpec((tm, LANES), row),
                  pl.BlockSpec((tm, LANES), row)],
        out_specs=tuple(pl.BlockSpec((tm, n), row) for n, _ in outs),
        compiler_params=_cparams(1),
        name="in_proj",
    )(x2, w, b, cos, sin)


def _to_key(v):
    bits = lax.bitcast_convert_type(v, jnp.int32)
    return bits ^ ((bits >> 31) & 0x7FFFFFFF)


def _dsa_kernel(qi_ref, qd_ref, kk_ref, dvt_ref, iwt_ref, o_ref,
                sc_ref, qip_ref, qdp_ref, r_ref, bias_ref, m_ref, l_ref, acc_ref, *, n_sel, row_sub):
    i = pl.program_id(1)
    tq = qi_ref.shape[1]
    tk = kk_ref.shape[2]
    n_tiles = i + 1
    idx_scale = (IDX_DIM ** -0.5) * (IDX_HEADS ** -0.5)

    for h in range(IDX_HEADS):
        grp = slice((h // 2) * LANES, (h // 2 + 1) * LANES)
        qi = qi_ref[0, :, grp].astype(F32)
        qd = qd_ref[0, :, grp].astype(F32)
        qip_ref[h * tq:(h + 1) * tq, :] = _move_half(qi, src_high=(h % 2 == 1), dst_high=True).astype(BF16)
        qdp_ref[h * tq:(h + 1) * tq, :] = _move_half(qd, src_high=(h % 2 == 1), dst_high=False).astype(BF16)

    w_rows = iwt_ref[0] * idx_scale
    t_base = i * tq

    def score_tile(j, carry):
        kt = kk_ref[0, j]
        r_ref[...] = lax.dot_general(kt, qip_ref[...], NT_DIMS, preferred_element_type=F32)
        for rs in range(tk // row_sub):
            rows = slice(rs * row_sub, (rs + 1) * row_sub)
            acc = jnp.zeros((row_sub, tq), F32)
            for h in range(IDX_HEADS):
                acc = acc + jnp.maximum(r_ref[rows, h * tq:(h + 1) * tq], 0.0) * w_rows[h:h + 1, :]
            key = _to_key(acc)
            d0 = (lax.broadcasted_iota(jnp.int32, (row_sub, tq), 0)
                  - lax.broadcasted_iota(jnp.int32, (row_sub, tq), 1))
            causal = d0 <= (t_base - j * tk - rs * row_sub)
            sc_ref[j, rows, :] = jnp.where(causal, key, INT_MIN)
        return carry

    lax.fori_loop(0, n_tiles, score_tile, 0)

    def count_ge(cand):
        def body(j, c):
            hit = jnp.where(sc_ref[j] >= cand, 1, 0)
            return c + jnp.sum(hit.reshape(tk // 8, 8, tq), axis=0)
        c8 = lax.fori_loop(0, n_tiles, body, jnp.zeros((8, tq), jnp.int32))
        return jnp.sum(c8, axis=0, keepdims=True)

    def search_bit(it, thr_u):
        bit = lax.shift_left(jnp.int32(1), 31 - it)
        cand_u = thr_u | bit
        cnt = count_ge(cand_u ^ INT_MIN)
        return jnp.where(cnt >= n_sel, cand_u, thr_u)

    thr_u = lax.fori_loop(0, 32, search_bit, jnp.zeros((1, tq), jnp.int32))
    thr = thr_u ^ INT_MIN

    cnt_ge = count_ge(thr)
    cnt_gt = count_ge(thr + 1)
    need = n_sel - cnt_gt
    has_tie = jnp.max(jnp.where((cnt_ge > n_sel) & (thr > KEY_NEG_INF), 1, 0)) > 0

    @pl.when(has_tie)
    def _():
        n_bits = int(sc_ref.shape[0] * tk - 1).bit_length()

        def count_eq_before(cut):
            def body(j, c):
                kidx = j * tk + lax.broadcasted_iota(jnp.int32, (tk, tq), 0)
                hit = jnp.where((sc_ref[j] == thr) & (kidx < cut), 1, 0)
                return c + jnp.sum(hit.reshape(tk // 8, 8, tq), axis=0)
            c8 = lax.fori_loop(0, n_tiles, body, jnp.zeros((8, tq), jnp.int32))
            return jnp.sum(c8, axis=0, keepdims=True)

        def cut_bit(it, cut):
            bit = lax.shift_left(jnp.int32(1), n_bits - 1 - it)
            cand = cut | bit
            return jnp.where(count_eq_before(cand) < need, cand, cut)

        cut = lax.fori_loop(0, n_bits, cut_bit, jnp.zeros((1, tq), jnp.int32))

        def demote(j, carry):
            kidx = j * tk + lax.broadcasted_iota(jnp.int32, (tk, tq), 0)
            k = sc_ref[j]
            sc_ref[j] = jnp.where((k == thr) & (kidx > cut) & (thr > KEY_NEG_INF), k - 1, k)
            return carry

        lax.fori_loop(0, n_tiles, demote, 0)

    thr_eff = jnp.maximum(thr, KEY_NEG_INF)

    m_ref[...] = jnp.full(m_ref.shape, MASK_VALUE, F32)
    l_ref[...] = jnp.zeros(l_ref.shape, F32)
    acc_ref[...] = jnp.zeros(acc_ref.shape, F32)

    def attend_tile(j, carry):
        kt = kk_ref[0, j]
        vt = dvt_ref[0, j]
        bias_ref[...] = jnp.where(sc_ref[j] >= thr_eff, 0.0, MASK_VALUE)
        for h in range(DSA_HEADS):
            s = lax.dot_general(kt, qdp_ref[h * tq:(h + 1) * tq, :], NT_DIMS,
                                preferred_element_type=F32) + bias_ref[...]
            m_old = m_ref[h:h + 1, :]
            m_new = jnp.maximum(m_old, jnp.max(s, axis=0, keepdims=True))
            p = jnp.exp(s - m_new)
            alpha = jnp.exp(m_old - m_new)
            l_ref[h:h + 1, :] = alpha * l_ref[h:h + 1, :] + jnp.sum(p, axis=0, keepdims=True)
            hd = slice(h * HEAD_DIM, (h + 1) * HEAD_DIM)
            acc_ref[hd, :] = alpha * acc_ref[hd, :] + jnp.dot(vt, p.astype(BF16), preferred_element_type=F32)
            m_ref[h:h + 1, :] = m_new
        return carry

    lax.fori_loop(0, n_tiles, attend_tile, 0)

    for h in range(DSA_HEADS):
        hd = slice(h * HEAD_DIM, (h + 1) * HEAD_DIM)
        acc_ref[hd, :] = acc_ref[hd, :] / l_ref[h:h + 1, :]
    o_ref[0] = jnp.transpose(acc_ref[...]).astype(o_ref.dtype)


def _dsa(qi, qd, kk4, dvt4, iwt, tq, n_sel):
    b, s, _ = qi.shape
    nt = s // tq
    row_sub = 64 if tq % 64 == 0 else tq
    kern = functools.partial(_dsa_kernel, n_sel=n_sel, row_sub=row_sub)
    return pl.pallas_call(
        kern,
        out_shape=jax.ShapeDtypeStruct((b, s, DSA_HEADS * HEAD_DIM), BF16),
        grid=(b, nt),
        in_specs=[pl.BlockSpec((1, tq, 512), lambda bi, i: (bi, i, 0)),
                  pl.BlockSpec((1, tq, 512), lambda bi, i: (bi, i, 0)),
                  pl.BlockSpec((1, nt, tq, LANES), lambda bi, i: (bi, 0, 0, 0)),
                  pl.BlockSpec((1, nt, HEAD_DIM, tq), lambda bi, i: (bi, 0, 0, 0)),
                  pl.BlockSpec((1, IDX_HEADS, tq), lambda bi, i: (bi, 0, i))],
        out_specs=pl.BlockSpec((1, tq, 512), lambda bi, i: (bi, i, 0)),
        scratch_shapes=[pltpu.VMEM((nt, tq, tq), jnp.int32),
                        pltpu.VMEM((IDX_HEADS * tq, LANES), BF16),
                        pltpu.VMEM((DSA_HEADS * tq, LANES), BF16),
                        pltpu.VMEM((tq, IDX_HEADS * tq), F32),
                        pltpu.VMEM((tq, tq), F32),
                        pltpu.VMEM((DSA_HEADS, tq), F32),
                        pltpu.VMEM((DSA_HEADS, tq), F32),
                        pltpu.VMEM((DSA_HEADS * HEAD_DIM, tq), F32)],
        compiler_params=_cparams(2),
        name="dsa",
    )(qi, qd, kk4, dvt4, iwt)


def _swa_kernel(sink_ref, q_ref, kp_ref, kc_ref, vp_ref, vc_ref, o_ref):
    n = pl.program_id(1)
    tb = q_ref.shape[1]
    kband = jnp.concatenate([kp_ref[0], kc_ref[0]], axis=0)
    vband = jnp.concatenate([vp_ref[0], vc_ref[0]], axis=0)
    r = lax.broadcasted_iota(jnp.int32, (tb, 2 * tb), 0)
    c = lax.broadcasted_iota(jnp.int32, (tb, 2 * tb), 1)
    diff = r + tb - c
    valid = (diff >= 0) & (diff < WINDOW) & ((c >= tb) | (n > 0))
    group = SWA_HEADS // SWA_KV_HEADS
    for pair in range(SWA_HEADS // 2):
        g = (2 * pair) // group
        qg = q_ref[0, :, pair * LANES:(pair + 1) * LANES].astype(F32)
        outs = []
        for par in range(2):
            h = 2 * pair + par
            qp = _move_half(qg, src_high=(par == 1), dst_high=(g == 1)).astype(BF16)
            s = lax.dot_general(qp, kband, NT_DIMS, preferred_element_type=F32)
            s = jnp.where(valid, s, -jnp.inf)
            sink = sink_ref[h]
            m = jnp.maximum(jnp.max(s, axis=-1, keepdims=True), sink)
            p = jnp.exp(s - m)
            den = jnp.sum(p, axis=-1, keepdims=True) + jnp.exp(sink - m)
            o = jnp.dot(p.astype(BF16), vband, preferred_element_type=F32) / den
            outs.append(_move_half(o, src_high=(g == 1), dst_high=(par == 1)))
        o_ref[0, :, pair * LANES:(pair + 1) * LANES] = (outs[0] + outs[1]).astype(o_ref.dtype)


def _swa(sinks, qs, sk, sv, tb):
    b, s, _ = qs.shape
    cur = lambda bi, n: (bi, n, 0)
    prev = lambda bi, n: (bi, jnp.maximum(n - 1, 0), 0)
    return pl.pallas_call(
        _swa_kernel,
        out_shape=jax.ShapeDtypeStruct((b, s, SWA_HEADS * HEAD_DIM), BF16),
        grid=(b, s // tb),
        in_specs=[pl.BlockSpec(memory_space=pltpu.SMEM),
                  pl.BlockSpec((1, tb, 512), cur),
                  pl.BlockSpec((1, tb, LANES), prev),
                  pl.BlockSpec((1, tb, LANES), cur),
                  pl.BlockSpec((1, tb, LANES), prev),
                  pl.BlockSpec((1, tb, LANES), cur)],
        out_specs=pl.BlockSpec((1, tb, 512), cur),
        compiler_params=_cparams(2),
        name="swa",
    )(sinks, qs, sk, sk, sv, sv)


def _outproj_kernel(x_ref, od_ref, os_ref, w_ref, g_ref, b_ref, h_ref):
    half = od_ref.shape[1]
    mix = (jnp.dot(od_ref[...], w_ref[:half, :], preferred_element_type=F32)
           + jnp.dot(os_ref[...], w_ref[half:, :], preferred_element_type=F32))
    h_ref[...] = _layer_norm(DEEPNORM_ALPHA * x_ref[...] + mix, g_ref[...], b_ref[...])


def _out_proj_ln(x2, o_dsa, o_swa, w_o, g, bta, tm):
    t, d = x2.shape
    row = lambda i: (i, 0)
    const = lambda i: (0, 0)
    return pl.pallas_call(
        _outproj_kernel,
        out_shape=jax.ShapeDtypeStruct((t, d), F32),
        grid=(t // tm,),
        in_specs=[pl.BlockSpec((tm, d), row),
                  pl.BlockSpec((tm, o_dsa.shape[1]), row),
                  pl.BlockSpec((tm, o_swa.shape[1]), row),
                  pl.BlockSpec(w_o.shape, const),
                  pl.BlockSpec((1, d), const),
                  pl.BlockSpec((1, d), const)],
        out_specs=pl.BlockSpec((tm, d), row),
        compiler_params=_cparams(1),
        name="out_proj_ln",
    )(x2, o_dsa, o_swa, w_o, g, bta)


def _memkv_kernel(mem_ref, wk_ref, wv_ref, k_ref, v_ref):
    m = mem_ref[...].astype(BF16)
    k_ref[...] = jnp.dot(m, wk_ref[...], preferred_element_type=F32).astype(k_ref.dtype)
    v_ref[...] = jnp.dot(m, wv_ref[...], preferred_element_type=F32).astype(v_ref.dtype)


def _mem_kv(mem2, wk, wv, tm):
    t, d = mem2.shape
    row = lambda i: (i, 0)
    const = lambda i: (0, 0)
    return pl.pallas_call(
        _memkv_kernel,
        out_shape=(jax.ShapeDtypeStruct((t, d), BF16), jax.ShapeDtypeStruct((t, d), BF16)),
        grid=(t // tm,),
        in_specs=[pl.BlockSpec((tm, d), row), pl.BlockSpec(wk.shape, const), pl.BlockSpec(wv.shape, const)],
        out_specs=(pl.BlockSpec((tm, d), row), pl.BlockSpec((tm, d), row)),
        compiler_params=_cparams(1),
        name="mem_kv",
    )(mem2, wk, wv)


def _cross_kernel(h_ref, wq_ref, km_ref, vm_ref, wo_ref, g_ref, b_ref, o_ref):
    hf = h_ref[0]
    d = hf.shape[1]
    hd = d // MEM_HEADS
    q = (jnp.dot(hf.astype(BF16), wq_ref[...], preferred_element_type=F32) * (hd ** -0.5)).astype(BF16)
    outs = []
    for hh in range(MEM_HEADS):
        cols = slice(hh * hd, (hh + 1) * hd)
        s = lax.dot_general(q[:, cols], km_ref[0, :, cols], NT_DIMS, preferred_element_type=F32)
        m = jnp.max(s, axis=-1, keepdims=True)
        p = jnp.exp(s - m)
        den = jnp.sum(p, axis=-1, keepdims=True)
        o = jnp.dot(p.astype(BF16), vm_ref[0, :, cols], preferred_element_type=F32) / den
        outs.append(o.astype(BF16))
    o_all = jnp.concatenate(outs, axis=-1)
    c = jnp.dot(o_all, wo_ref[...], preferred_element_type=F32)
    o_ref[0] = _layer_norm(DEEPNORM_ALPHA * hf + c, g_ref[...], b_ref[...])


def _cross_ln(h3, wq, km, vm, wo, g, bta, tm):
    b, s, d = h3.shape
    m = km.shape[1]
    row = lambda bi, i: (bi, i, 0)
    per_b = lambda bi, i: (bi, 0, 0)
    const = lambda bi, i: (0, 0)
    return pl.pallas_call(
        _cross_kernel,
        out_shape=jax.ShapeDtypeStruct((b, s, d), F32),
        grid=(b, s // tm),
        in_specs=[pl.BlockSpec((1, tm, d), row),
                  pl.BlockSpec(wq.shape, const),
                  pl.BlockSpec((1, m, d), per_b),
                  pl.BlockSpec((1, m, d), per_b),
                  pl.BlockSpec(wo.shape, const),
                  pl.BlockSpec((1, d), const),
                  pl.BlockSpec((1, d), const)],
        out_specs=pl.BlockSpec((1, tm, d), row),
        compiler_params=_cparams(2),
        name="cross_ln",
    )(h3, wq, km, vm, wo, g, bta)


def _mlp_kernel(h_ref, wu_ref, wd_ref, g_ref, b_ref, o_ref, *, f_chunk):
    hf = h_ref[...]
    hb = hf.astype(BF16)
    acc = DEEPNORM_ALPHA * hf
    for c0 in range(0, wu_ref.shape[1], f_chunk):
        u = jnp.maximum(jnp.dot(hb, wu_ref[:, c0:c0 + f_chunk], preferred_element_type=F32), 0.0)
        acc = acc + jnp.dot((u * u).astype(BF16), wd_ref[c0:c0 + f_chunk, :], preferred_element_type=F32)
    o_ref[...] = _layer_norm(acc, g_ref[...], b_ref[...])


def _mlp_ln(h2, w_up, w_down, g, bta, tm):
    t, d = h2.shape
    row = lambda i: (i, 0)
    const = lambda i: (0, 0)
    once = pl.Buffered(1)
    return pl.pallas_call(
        functools.partial(_mlp_kernel, f_chunk=1024),
        out_shape=jax.ShapeDtypeStruct((t, d), F32),
        grid=(t // tm,),
        in_specs=[pl.BlockSpec((tm, d), row),
                  pl.BlockSpec(w_up.shape, const, pipeline_mode=once),
                  pl.BlockSpec(w_down.shape, const, pipeline_mode=once),
                  pl.BlockSpec((1, d), const),
                  pl.BlockSpec((1, d), const)],
        out_specs=pl.BlockSpec((tm, d), row),
        compiler_params=_cparams(1),
        name="mlp_ln",
    )(h2, w_up, w_down, g, bta)


def _tile(n, pref):
    while n % pref:
        pref //= 2
    return pref


def kernel(x, mem, positions, w_in, b_in, swa_sinks, w_o, ln1_g, ln1_b, wq_mem, wk_mem, wv_mem, wo_mem,
           ln2_g, ln2_b, w_up, w_down, ln3_g, ln3_b):
    b, s, d = x.shape
    t = b * s
    assert w_in.shape[0] == DEPTH and s % LANES == 0
    n_sel = min(TOPK_MAX, s // 4)

    inv = ROPE_THETA ** (-jnp.arange(0, HEAD_DIM, 2, dtype=F32) / HEAD_DIM)
    inv_row = jnp.tile(inv, LANES // inv.shape[0])[None, :]
    sgn_row = jnp.where((jnp.arange(LANES) % HEAD_DIM) < HEAD_DIM // 2, -1.0, 1.0).astype(F32)[None, :]
    sizes = (DSA_HEADS * HEAD_DIM, HEAD_DIM, HEAD_DIM, IDX_HEADS * IDX_DIM, IDX_DIM, IDX_HEADS,
             SWA_HEADS * HEAD_DIM, SWA_KV_HEADS * HEAD_DIM, SWA_KV_HEADS * HEAD_DIM)
    off = np.concatenate([[0], np.cumsum(sizes)])
    seg = lambda k: np.arange(off[k], off[k + 1])
    perm = np.concatenate([seg(0), seg(3), seg(6), seg(7), seg(1), seg(4), seg(8), seg(2), seg(5)])
    pad = C_END - perm.shape[0]
    w_p = jnp.pad(w_in[0][:, perm], ((0, 0), (0, pad))).astype(BF16)
    b_p = jnp.pad(b_in[0][perm], (0, pad))[None, :]

    x2 = x.reshape(t, d)
    cos, sin = _rope_tables(positions.reshape(t, 1), inv_row, sgn_row, _tile(t, 2048))
    qd, qi, qs, sk, kk, sv, misc = _in_proj(x2, w_p, b_p, cos, sin, _tile(t, 512))

    tq = _tile(s, 256)
    nt = s // tq
    dvt4 = jnp.swapaxes(misc[:, :HEAD_DIM].astype(BF16).reshape(b, nt, tq, HEAD_DIM), 2, 3)
    iwt = jnp.swapaxes(misc[:, HEAD_DIM:HEAD_DIM + IDX_HEADS].reshape(b, s, IDX_HEADS), 1, 2)
    o_dsa = _dsa(qi.reshape(b, s, -1), qd.reshape(b, s, -1), kk.reshape(b, nt, tq, LANES), dvt4, iwt, tq, n_sel)

    o_swa = _swa(swa_sinks[0], qs.reshape(b, s, -1), sk.reshape(b, s, LANES), sv.reshape(b, s, LANES), WINDOW)

    h1 = _out_proj_ln(x2, o_dsa.reshape(t, -1), o_swa.reshape(t, -1), w_o[0].astype(BF16),
                      ln1_g[0][None, :], ln1_b[0][None, :], _tile(t, 512))

    mt = mem.shape[1]
    km, vm = _mem_kv(mem.reshape(b * mt, d), wk_mem[0].astype(BF16), wv_mem[0].astype(BF16), _tile(b * mt, 256))
    h2 = _cross_ln(h1.reshape(b, s, d), wq_mem[0].astype(BF16), km.reshape(b, mt, d), vm.reshape(b, mt, d),
                   wo_mem[0].astype(BF16), ln2_g[0][None, :], ln2_b[0][None, :], _tile(s, 256))

    h3 = _mlp_ln(h2.reshape(t, d), w_up[0].astype(BF16), w_down[0].astype(BF16),
                 ln3_g[0][None, :], ln3_b[0][None, :], _tile(t, 512))
    return h3.reshape(b, s, d)
```

```python
import functools
import math

import jax
import jax.numpy as jnp
import numpy as np
from jax import lax
from jax.experimental import pallas as pl
from jax.experimental.pallas import tpu as pltpu

HEAD_DIM = 64
DSA_HEADS = 8
IDX_HEADS = 8
IDX_DIM = 64
TOPK_MAX = 256
SWA_HEADS = 8
SWA_KV_HEADS = 2
WINDOW = 128
MEM_HEADS = 4
ROPE_THETA = 10000.0
LN_EPS = 1e-5
DEPTH = 1
DEEPNORM_ALPHA = (2.0 * DEPTH) ** 0.25

LANES = 128
PACK16 = 16
VMEM_LIMIT = 56 * 1024 * 1024
INT_MIN = -(2 ** 31)
KEY_NEG_INF = (0xFF800000 ^ 0x7FFFFFFF) - (1 << 32)
MASK_VALUE = -1e30
LOG2E = math.log2(math.e)
V_ROWS = HEAD_DIM + PACK16

F32 = jnp.float32
BF16 = jnp.bfloat16
I16 = jnp.int16
NT_DIMS = (((1,), (1,)), ((), ()))


def _cparams(n_axes):
    return pltpu.CompilerParams(
        dimension_semantics=("arbitrary",) * n_axes, vmem_limit_bytes=VMEM_LIMIT)


def _layer_norm(y, g, b):
    mu = jnp.mean(y, axis=-1, keepdims=True)
    d = y - mu
    var = jnp.mean(d * d, axis=-1, keepdims=True)
    return d * lax.rsqrt(var + LN_EPS) * g + b


def _half_mask(shape, low):
    lane = lax.broadcasted_iota(jnp.int32, shape, len(shape) - 1) % LANES
    return (lane < HEAD_DIM) if low else (lane >= HEAD_DIM)


def _move_half(v, src_high, dst_high):
    if src_high != dst_high:
        v = pltpu.roll(v, HEAD_DIM, 1)
    return jnp.where(_half_mask(v.shape, low=not dst_high), v, 0.0)


def _rope_table_kernel(pos_ref, inv_ref, sgn_ref, cos_ref, sin_ref):
    ang = pos_ref[...].astype(F32) * inv_ref[...]
    cos_ref[...] = jnp.cos(ang)
    sin_ref[...] = jnp.sin(ang) * sgn_ref[...]


def _rope_tables(pos_col, inv_row, sgn_row, tm):
    t = pos_col.shape[0]
    return pl.pallas_call(
        _rope_table_kernel,
        out_shape=(jax.ShapeDtypeStruct((t, LANES), F32), jax.ShapeDtypeStruct((t, LANES), F32)),
        grid=(t // tm,),
        in_specs=[pl.BlockSpec((tm, 1), lambda i: (i, 0)),
                  pl.BlockSpec((1, LANES), lambda i: (0, 0)),
                  pl.BlockSpec((1, LANES), lambda i: (0, 0))],
        out_specs=(pl.BlockSpec((tm, LANES), lambda i: (i, 0)),
                   pl.BlockSpec((tm, LANES), lambda i: (i, 0))),
        compiler_params=_cparams(1),
        name="rope_tables",
    )(pos_col, inv_row, sgn_row)


C_DQ, C_IQ, C_SQ, C_SK, C_KK, C_SV, C_MISC, C_END = 0, 512, 1024, 1536, 1664, 1792, 1920, 2048


def _inproj_kernel(x_ref, w_ref, b_ref, cos_ref, sin_ref,
                   qd_ref, qi_ref, qs_ref, sk_ref, kk_ref, sv_ref, misc_ref):
    x = x_ref[...].astype(BF16)
    cos = cos_ref[...]
    sin = sin_ref[...]
    first = (lax.broadcasted_iota(jnp.int32, cos.shape, 1) % HEAD_DIM) < (HEAD_DIM // 2)

    def proj(c0, c1):
        return jnp.dot(x, w_ref[:, c0:c1], preferred_element_type=F32) + b_ref[:, c0:c1]

    def rope(v):
        sw = jnp.where(first, pltpu.roll(v, LANES - HEAD_DIM // 2, 1), pltpu.roll(v, HEAD_DIM // 2, 1))
        return v * cos + sw * sin

    def rope_store(out_ref, c0, c1, scale):
        y = proj(c0, c1)
        for g in range((c1 - c0) // LANES):
            r = rope(y[:, g * LANES:(g + 1) * LANES])
            if scale != 1.0:
                r = r * scale
            out_ref[:, g * LANES:(g + 1) * LANES] = r.astype(out_ref.dtype)

    scale = HEAD_DIM ** -0.5
    rope_store(qd_ref, C_DQ, C_IQ, scale * LOG2E)
    rope_store(qi_ref, C_IQ, C_SQ, 1.0)
    rope_store(qs_ref, C_SQ, C_SK, scale)
    rope_store(sk_ref, C_SK, C_KK, 1.0)
    rope_store(kk_ref, C_KK, C_SV, 1.0)
    sv_ref[...] = proj(C_SV, C_MISC).astype(sv_ref.dtype)
    misc_ref[...] = proj(C_MISC, C_END)


def _in_proj(x2, w, b, cos, sin, tm):
    t, d = x2.shape
    row = lambda i: (i, 0)
    const = lambda i: (0, 0)
    outs = [(512, BF16), (512, BF16), (512, BF16), (128, BF16), (128, BF16), (128, BF16), (128, F32)]
    return pl.pallas_call(
        _inproj_kernel,
        out_shape=tuple(jax.ShapeDtypeStruct((t, n), dt) for n, dt in outs),
        grid=(t // tm,),
        in_specs=[pl.BlockSpec((tm, d), row),
                  pl.BlockSpec((d, C_END), const),
                  pl.BlockSpec((1, C_END), const),
                  pl.BlockSpec((tm, LANES), row),
                  pl.BlockSpec((tm, LANES), row)],
        out_specs=tuple(pl.BlockSpec((tm, n), row) for n, _ in outs),
        compiler_params=_cparams(1),
        name="in_proj",
    )(x2, w, b, cos, sin)


def _to_key(v):
    bits = lax.bitcast_convert_type(v, jnp.int32)
    return bits ^ ((bits >> 31) & 0x7FFFFFFF)


def _dsa_kernel(qi_ref, qd_ref, kk_ref, dvt_ref, iwt_ref, o_ref,
                sc_ref, hi_ref, lo_ref, qip_ref, qdp_ref, r_ref, p_ref, bias_ref, m_ref, acc_ref, ot_ref,
                *, n_sel, row_sub):
    i = pl.program_id(1)
    tq = qi_ref.shape[1]
    tk = kk_ref.shape[2]
    n_tiles = i + 1
    n_sub = tk // row_sub
    idx_scale = (IDX_DIM ** -0.5) * (IDX_HEADS ** -0.5)

    for h in range(IDX_HEADS):
        grp = slice((h // 2) * LANES, (h // 2 + 1) * LANES)
        qi = qi_ref[0, :, grp].astype(F32)
        qd = qd_ref[0, :, grp].astype(F32)
        qip_ref[h * tq:(h + 1) * tq, :] = _move_half(qi, src_high=(h % 2 == 1), dst_high=True).astype(BF16)
        qdp_ref[h * tq:(h + 1) * tq, :] = _move_half(qd, src_high=(h % 2 == 1), dst_high=False).astype(BF16)

    w_rows = iwt_ref[0] * idx_scale
    t_base = i * tq

    def score_tile(j, carry):
        kt = kk_ref[0, j]
        r_ref[...] = lax.dot_general(kt, qip_ref[...], NT_DIMS, preferred_element_type=F32)
        for rs in range(n_sub):
            rows = slice(rs * row_sub, (rs + 1) * row_sub)
            acc = jnp.zeros((row_sub, tq), F32)
            for h in range(IDX_HEADS):
                acc = acc + jnp.maximum(r_ref[rows, h * tq:(h + 1) * tq], 0.0) * w_rows[h:h + 1, :]
            d0 = (lax.broadcasted_iota(jnp.int32, (row_sub, tq), 0)
                  - lax.broadcasted_iota(jnp.int32, (row_sub, tq), 1))
            causal = d0 <= (t_base - j * tk - rs * row_sub)
            key = jnp.where(causal, _to_key(acc), INT_MIN)
            sc_ref[j, rows, :] = key
            hi_ref[j, rows, :] = (key >> 16).astype(I16)
            lo_ref[j, rows, :] = ((key & 0xFFFF) - 32768).astype(I16)
        return carry

    lax.fori_loop(0, n_tiles, score_tile, 0)

    def search16(ref):
        def count_ge16(cand):
            def body(j, c):
                hit = jnp.where(ref[j] >= cand, jnp.ones((), I16), jnp.zeros((), I16))
                for g in range(tk // PACK16):
                    c = c + hit[g * PACK16:(g + 1) * PACK16]
                return c
            c16 = lax.fori_loop(0, n_tiles, body, jnp.zeros((PACK16, tq), I16))
            return jnp.sum(c16.astype(jnp.int32), axis=0, keepdims=True)

        def bit_step(it, t_u):
            cand_u = t_u | lax.shift_left(jnp.int32(1), 15 - it)
            cnt = count_ge16((cand_u - 32768).astype(I16))
            return jnp.where(cnt >= n_sel, cand_u, t_u)

        return lax.fori_loop(0, 16, bit_step, jnp.zeros((1, tq), jnp.int32))

    thi = search16(hi_ref) - 32768
    thi16 = thi.astype(I16)

    def fold_hi(j, carry):
        hi = hi_ref[j]
        sat = jnp.where(hi > thi16, jnp.full((), 32767, I16), jnp.full((), -32768, I16))
        lo_ref[j] = jnp.where(hi == thi16, lo_ref[j], sat)
        return carry

    lax.fori_loop(0, n_tiles, fold_hi, 0)
    thr = thi * 65536 + search16(lo_ref)

    def count_ge(cand):
        def body(j, c):
            hit = jnp.where(sc_ref[j] >= cand, 1, 0)
            return c + jnp.sum(hit.reshape(tk // 8, 8, tq), axis=0)
        c8 = lax.fori_loop(0, n_tiles, body, jnp.zeros((8, tq), jnp.int32))
        return jnp.sum(c8, axis=0, keepdims=True)

    cnt_ge = count_ge(thr)
    cnt_gt = count_ge(thr + 1)
    need = n_sel - cnt_gt
    has_tie = jnp.max(jnp.where((cnt_ge > n_sel) & (thr > KEY_NEG_INF), 1, 0)) > 0

    @pl.when(has_tie)
    def _():
        n_bits = int(sc_ref.shape[0] * tk - 1).bit_length()

        def count_eq_before(cut):
            def body(j, c):
                kidx = j * tk + lax.broadcasted_iota(jnp.int32, (tk, tq), 0)
                hit = jnp.where((sc_ref[j] == thr) & (kidx < cut), 1, 0)
                return c + jnp.sum(hit.reshape(tk // 8, 8, tq), axis=0)
            c8 = lax.fori_loop(0, n_tiles, body, jnp.zeros((8, tq), jnp.int32))
            return jnp.sum(c8, axis=0, keepdims=True)

        def cut_bit(it, cut):
            bit = lax.shift_left(jnp.int32(1), n_bits - 1 - it)
            cand = cut | bit
            return jnp.where(count_eq_before(cand) < need, cand, cut)

        cut = lax.fori_loop(0, n_bits, cut_bit, jnp.zeros((1, tq), jnp.int32))

        def demote(j, carry):
            kidx = j * tk + lax.broadcasted_iota(jnp.int32, (tk, tq), 0)
            k = sc_ref[j]
            sc_ref[j] = jnp.where((k == thr) & (kidx > cut) & (thr > KEY_NEG_INF), k - 1, k)
            return carry

        lax.fori_loop(0, n_tiles, demote, 0)

    thr_eff = jnp.maximum(thr, KEY_NEG_INF)

    m_ref[...] = jnp.full(m_ref.shape, MASK_VALUE, F32)
    acc_ref[...] = jnp.zeros(acc_ref.shape, F32)

    def attend_tile(j, carry):
        kt = kk_ref[0, j]
        vt = dvt_ref[0, j]
        bias_ref[...] = jnp.where(sc_ref[j] >= thr_eff, 0.0, MASK_VALUE)
        r_ref[...] = lax.dot_general(kt, qdp_ref[...], NT_DIMS, preferred_element_type=F32)
        for h in range(DSA_HEADS):
            cols = slice(h * tq, (h + 1) * tq)
            m8 = jnp.full((8, tq), MASK_VALUE, F32)
            for rs in range(n_sub):
                rows = slice(rs * row_sub, (rs + 1) * row_sub)
                sm = r_ref[rows, cols] + bias_ref[rows, :]
                r_ref[rows, cols] = sm
                m8 = jnp.maximum(m8, jnp.max(sm.reshape(row_sub // 8, 8, tq), axis=0))
            m_old = m_ref[h:h + 1, :]
            m_new = jnp.maximum(m_old, jnp.max(m8, axis=0, keepdims=True))
            alpha = jnp.exp2(m_old - m_new)
            for rs in range(n_sub):
                rows = slice(rs * row_sub, (rs + 1) * row_sub)
                p_ref[rows, cols] = jnp.exp2(r_ref[rows, cols] - m_new).astype(BF16)
            hd = slice(h * V_ROWS, (h + 1) * V_ROWS)
            acc_ref[hd, :] = alpha * acc_ref[hd, :] + jnp.dot(vt, p_ref[:, cols], preferred_element_type=F32)
            m_ref[h:h + 1, :] = m_new
        return carry

    lax.fori_loop(0, n_tiles, attend_tile, 0)

    for h in range(DSA_HEADS):
        den = acc_ref[h * V_ROWS + HEAD_DIM:h * V_ROWS + HEAD_DIM + 1, :]
        ot_ref[h * HEAD_DIM:(h + 1) * HEAD_DIM, :] = acc_ref[h * V_ROWS:h * V_ROWS + HEAD_DIM, :] / den
    o_ref[0] = jnp.transpose(ot_ref[...]).astype(o_ref.dtype)


def _dsa(qi, qd, kk4, dvt4, iwt, tq, n_sel):
    b, s, _ = qi.shape
    nt = s // tq
    row_sub = 64 if tq % 64 == 0 else tq
    kern = functools.partial(_dsa_kernel, n_sel=n_sel, row_sub=row_sub)
    return pl.pallas_call(
        kern,
        out_shape=jax.ShapeDtypeStruct((b, s, DSA_HEADS * HEAD_DIM), BF16),
        grid=(b, nt),
        in_specs=[pl.BlockSpec((1, tq, 512), lambda bi, i: (bi, i, 0)),
                  pl.BlockSpec((1, tq, 512), lambda bi, i: (bi, i, 0)),
                  pl.BlockSpec((1, nt, tq, LANES), lambda bi, i: (bi, 0, 0, 0)),
                  pl.BlockSpec((1, nt, V_ROWS, tq), lambda bi, i: (bi, 0, 0, 0)),
                  pl.BlockSpec((1, IDX_HEADS, tq), lambda bi, i: (bi, 0, i))],
        out_specs=pl.BlockSpec((1, tq, 512), lambda bi, i: (bi, i, 0)),
        scratch_shapes=[pltpu.VMEM((nt, tq, tq), jnp.int32),
                        pltpu.VMEM((nt, tq, tq), I16),
                        pltpu.VMEM((nt, tq, tq), I16),
                        pltpu.VMEM((IDX_HEADS * tq, LANES), BF16),
                        pltpu.VMEM((DSA_HEADS * tq, LANES), BF16),
                        pltpu.VMEM((tq, IDX_HEADS * tq), F32),
                        pltpu.VMEM((tq, DSA_HEADS * tq), BF16),
                        pltpu.VMEM((tq, tq), F32),
                        pltpu.VMEM((DSA_HEADS, tq), F32),
                        pltpu.VMEM((DSA_HEADS * V_ROWS, tq), F32),
                        pltpu.VMEM((DSA_HEADS * HEAD_DIM, tq), F32)],
        compiler_params=_cparams(2),
        name="dsa",
    )(qi, qd, kk4, dvt4, iwt)


def _swa_kernel(sink_ref, q_ref, kp_ref, kc_ref, vp_ref, vc_ref, o_ref):
    n = pl.program_id(1)
    tq = q_ref.shape[1]
    tb = WINDOW
    group = SWA_HEADS // SWA_KV_HEADS
    kall = jnp.concatenate([kp_ref[0], kc_ref[0]], axis=0)
    vall = jnp.concatenate([vp_ref[0], vc_ref[0]], axis=0)
    shape = (group * tb, 2 * tb)
    r = lax.broadcasted_iota(jnp.int32, shape, 0) & (tb - 1)
    c = lax.broadcasted_iota(jnp.int32, shape, 1)
    diff = r + tb - c
    inband = (diff >= 0) & (diff < WINDOW)
    for u in range(tq // tb):
        kband = kall[u * tb:(u + 2) * tb]
        vband = vall[u * tb:(u + 2) * tb]
        valid = (inband & ((c >= tb) | (n > 0))) if u == 0 else inband
        for g in range(SWA_KV_HEADS):
            heads = range(g * group, (g + 1) * group)
            qs = []
            for h in heads:
                qg = q_ref[0, u * tb:(u + 1) * tb, (h // 2) * LANES:(h // 2 + 1) * LANES].astype(F32)
                qs.append(_move_half(qg, src_high=(h % 2 == 1), dst_high=(g == 1)).astype(BF16))
            s = lax.dot_general(jnp.concatenate(qs, axis=0), kband, NT_DIMS,
                                preferred_element_type=F32)
            s = jnp.where(valid, s, -jnp.inf)
            sink = jnp.concatenate([jnp.full((tb, 1), sink_ref[h], F32) for h in heads], axis=0)
            m = jnp.maximum(jnp.max(s, axis=-1, keepdims=True), sink)
            p = jnp.exp(s - m)
            den = jnp.sum(p, axis=-1, keepdims=True) + jnp.exp(sink - m)
            o = jnp.dot(p.astype(BF16), vband, preferred_element_type=F32) / den
            for pair in range(group // 2):
                h0 = g * group + 2 * pair
                lo = _move_half(o[(2 * pair) * tb:(2 * pair + 1) * tb], src_high=(g == 1), dst_high=False)
                hi = _move_half(o[(2 * pair + 1) * tb:(2 * pair + 2) * tb], src_high=(g == 1), dst_high=True)
                o_ref[0, u * tb:(u + 1) * tb, (h0 // 2) * LANES:(h0 // 2 + 1) * LANES] = (lo + hi).astype(o_ref.dtype)


def _swa(sinks, qs, sk, sv, tq):
    b, s, _ = qs.shape
    ratio = tq // WINDOW
    cur = lambda bi, n: (bi, n, 0)
    prev = lambda bi, n: (bi, jnp.maximum(n * ratio - 1, 0), 0)
    return pl.pallas_call(
        _swa_kernel,
        out_shape=jax.ShapeDtypeStruct((b, s, SWA_HEADS * HEAD_DIM), BF16),
        grid=(b, s // tq),
        in_specs=[pl.BlockSpec(memory_space=pltpu.SMEM),
                  pl.BlockSpec((1, tq, 512), cur),
                  pl.BlockSpec((1, WINDOW, LANES), prev),
                  pl.BlockSpec((1, tq, LANES), cur),
                  pl.BlockSpec((1, WINDOW, LANES), prev),
                  pl.BlockSpec((1, tq, LANES), cur)],
        out_specs=pl.BlockSpec((1, tq, 512), cur),
        compiler_params=_cparams(2),
        name="swa",
    )(sinks, qs, sk, sk, sv, sv)


def _outproj_kernel(x_ref, od_ref, os_ref, w_ref, g_ref, b_ref, h_ref):
    half = od_ref.shape[1]
    mix = (jnp.dot(od_ref[...], w_ref[:half, :], preferred_element_type=F32)
           + jnp.dot(os_ref[...], w_ref[half:, :], preferred_element_type=F32))
    h_ref[...] = _layer_norm(DEEPNORM_ALPHA * x_ref[...] + mix, g_ref[...], b_ref[...])


def _out_proj_ln(x2, o_dsa, o_swa, w_o, g, bta, tm):
    t, d = x2.shape
    row = lambda i: (i, 0)
    const = lambda i: (0, 0)
    return pl.pallas_call(
        _outproj_kernel,
        out_shape=jax.ShapeDtypeStruct((t, d), F32),
        grid=(t // tm,),
        in_specs=[pl.BlockSpec((tm, d), row),
                  pl.BlockSpec((tm, o_dsa.shape[1]), row),
                  pl.BlockSpec((tm, o_swa.shape[1]), row),
                  pl.BlockSpec(w_o.shape, const),
                  pl.BlockSpec((1, d), const),
                  pl.BlockSpec((1, d), const)],
        out_specs=pl.BlockSpec((tm, d), row),
        compiler_params=_cparams(1),
        name="out_proj_ln",
    )(x2, o_dsa, o_swa, w_o, g, bta)


def _memkv_kernel(mem_ref, wk_ref, wv_ref, k_ref, v_ref):
    m = mem_ref[...].astype(BF16)
    k_ref[...] = jnp.dot(m, wk_ref[...], preferred_element_type=F32).astype(k_ref.dtype)
    v_ref[...] = jnp.dot(m, wv_ref[...], preferred_element_type=F32).astype(v_ref.dtype)


def _mem_kv(mem2, wk, wv, tm):
    t, d = mem2.shape
    row = lambda i: (i, 0)
    const = lambda i: (0, 0)
    return pl.pallas_call(
        _memkv_kernel,
        out_shape=(jax.ShapeDtypeStruct((t, d), BF16), jax.ShapeDtypeStruct((t, d), BF16)),
        grid=(t // tm,),
        in_specs=[pl.BlockSpec((tm, d), row), pl.BlockSpec(wk.shape, const), pl.BlockSpec(wv.shape, const)],
        out_specs=(pl.BlockSpec((tm, d), row), pl.BlockSpec((tm, d), row)),
        compiler_params=_cparams(1),
        name="mem_kv",
    )(mem2, wk, wv)


def _cross_kernel(h_ref, wq_ref, km_ref, vm_ref, wo_ref, g_ref, b_ref, o_ref):
    hf = h_ref[0]
    d = hf.shape[1]
    hd = d // MEM_HEADS
    q = (jnp.dot(hf.astype(BF16), wq_ref[...], preferred_element_type=F32) * (hd ** -0.5)).astype(BF16)
    outs = []
    for hh in range(MEM_HEADS):
        cols = slice(hh * hd, (hh + 1) * hd)
        s = lax.dot_general(q[:, cols], km_ref[0, :, cols], NT_DIMS, preferred_element_type=F32)
        m = jnp.max(s, axis=-1, keepdims=True)
        p = jnp.exp(s - m)
        den = jnp.sum(p, axis=-1, keepdims=True)
        o = jnp.dot(p.astype(BF16), vm_ref[0, :, cols], preferred_element_type=F32) / den
        outs.append(o.astype(BF16))
    o_all = jnp.concatenate(outs, axis=-1)
    c = jnp.dot(o_all, wo_ref[...], preferred_element_type=F32)
    o_ref[0] = _layer_norm(DEEPNORM_ALPHA * hf + c, g_ref[...], b_ref[...])


def _cross_ln(h3, wq, km, vm, wo, g, bta, tm):
    b, s, d = h3.shape
    m = km.shape[1]
    row = lambda bi, i: (bi, i, 0)
    per_b = lambda bi, i: (bi, 0, 0)
    const = lambda bi, i: (0, 0)
    return pl.pallas_call(
        _cross_kernel,
        out_shape=jax.ShapeDtypeStruct((b, s, d), F32),
        grid=(b, s // tm),
        in_specs=[pl.BlockSpec((1, tm, d), row),
                  pl.BlockSpec(wq.shape, const),
                  pl.BlockSpec((1, m, d), per_b),
                  pl.BlockSpec((1, m, d), per_b),
                  pl.BlockSpec(wo.shape, const),
                  pl.BlockSpec((1, d), const),
                  pl.BlockSpec((1, d), const)],
        out_specs=pl.BlockSpec((1, tm, d), row),
        compiler_params=_cparams(2),
        name="cross_ln",
    )(h3, wq, km, vm, wo, g, bta)


def _mlp_kernel(h_ref, wu_ref, wd_ref, g_ref, b_ref, o_ref, *, f_chunk):
    hf = h_ref[...]
    hb = hf.astype(BF16)
    acc = DEEPNORM_ALPHA * hf
    for c0 in range(0, wu_ref.shape[1], f_chunk):
        u = jnp.maximum(jnp.dot(hb, wu_ref[:, c0:c0 + f_chunk], preferred_element_type=F32), 0.0)
        acc = acc + jnp.dot((u * u).astype(BF16), wd_ref[c0:c0 + f_chunk, :], preferred_element_type=F32)
    o_ref[...] = _layer_norm(acc, g_ref[...], b_ref[...])


def _mlp_ln(h2, w_up, w_down, g, bta, tm):
    t, d = h2.shape
    row = lambda i: (i, 0)
    const = lambda i: (0, 0)
    once = pl.Buffered(1)
    return pl.pallas_call(
        functools.partial(_mlp_kernel, f_chunk=1024),
        out_shape=jax.ShapeDtypeStruct((t, d), F32),
        grid=(t // tm,),
        in_specs=[pl.BlockSpec((tm, d), row),
                  pl.BlockSpec(w_up.shape, const, pipeline_mode=once),
                  pl.BlockSpec(w_down.shape, const, pipeline_mode=once),
                  pl.BlockSpec((1, d), const),
                  pl.BlockSpec((1, d), const)],
        out_specs=pl.BlockSpec((tm, d), row),
        compiler_params=_cparams(1),
        name="mlp_ln",
    )(h2, w_up, w_down, g, bta)


def _tile(n, pref):
    while n % pref:
        pref //= 2
    return pref


def kernel(x, mem, positions, w_in, b_in, swa_sinks, w_o, ln1_g, ln1_b, wq_mem, wk_mem, wv_mem, wo_mem,
           ln2_g, ln2_b, w_up, w_down, ln3_g, ln3_b):
    b, s, d = x.shape
    t = b * s
    assert w_in.shape[0] == DEPTH and s % LANES == 0
    n_sel = min(TOPK_MAX, s // 4)

    inv = ROPE_THETA ** (-jnp.arange(0, HEAD_DIM, 2, dtype=F32) / HEAD_DIM)
    inv_row = jnp.tile(inv, LANES // inv.shape[0])[None, :]
    sgn_row = jnp.where((jnp.arange(LANES) % HEAD_DIM) < HEAD_DIM // 2, -1.0, 1.0).astype(F32)[None, :]
    sizes = (DSA_HEADS * HEAD_DIM, HEAD_DIM, HEAD_DIM, IDX_HEADS * IDX_DIM, IDX_DIM, IDX_HEADS,
             SWA_HEADS * HEAD_DIM, SWA_KV_HEADS * HEAD_DIM, SWA_KV_HEADS * HEAD_DIM)
    off = np.concatenate([[0], np.cumsum(sizes)])
    order = (0, 3, 6, 7, 1, 4, 8, 2, 5)
    pad = C_END - int(off[-1])
    w_p = jnp.concatenate([w_in[0][:, off[k]:off[k + 1]] for k in order]
                          + [jnp.zeros((d, pad), w_in.dtype)], axis=1).astype(BF16)
    b_p = jnp.concatenate([b_in[0][off[k]:off[k + 1]] for k in order] + [jnp.zeros((pad,), b_in.dtype)])[None, :]

    x2 = x.reshape(t, d)
    cos, sin = _rope_tables(positions.reshape(t, 1), inv_row, sgn_row, _tile(t, 2048))
    qd, qi, qs, sk, kk, sv, misc = _in_proj(x2, w_p, b_p, cos, sin, _tile(t, 512))

    tq = _tile(s, 256)
    nt = s // tq
    dvt = jnp.swapaxes(misc[:, :HEAD_DIM].astype(BF16).reshape(b, nt, tq, HEAD_DIM), 2, 3)
    dvt4 = jnp.concatenate([dvt, jnp.ones((b, nt, V_ROWS - HEAD_DIM, tq), BF16)], axis=2)
    iwt = jnp.swapaxes(misc[:, HEAD_DIM:HEAD_DIM + IDX_HEADS].reshape(b, s, IDX_HEADS), 1, 2)
    o_dsa = _dsa(qi.reshape(b, s, -1), qd.reshape(b, s, -1), kk.reshape(b, nt, tq, LANES), dvt4, iwt, tq, n_sel)

    o_swa = _swa(swa_sinks[0], qs.reshape(b, s, -1), sk.reshape(b, s, LANES), sv.reshape(b, s, LANES),
                 _tile(s, 512))

    h1 = _out_proj_ln(x2, o_dsa.reshape(t, -1), o_swa.reshape(t, -1), w_o[0].astype(BF16),
                      ln1_g[0][None, :], ln1_b[0][None, :], _tile(t, 512))

    mt = mem.shape[1]
    km, vm = _mem_kv(mem.reshape(b * mt, d), wk_mem[0].astype(BF16), wv_mem[0].astype(BF16), _tile(b * mt, 256))
    h2 = _cross_ln(h1.reshape(b, s, d), wq_mem[0].astype(BF16), km.reshape(b, mt, d), vm.reshape(b, mt, d),
                   wo_mem[0].astype(BF16), ln2_g[0][None, :], ln2_b[0][None, :], _tile(s, 256))

    h3 = _mlp_ln(h2.reshape(t, d), w_up[0].astype(BF16), w_down[0].astype(BF16),
                 ln3_g[0][None, :], ln3_b[0][None, :], _tile(t, 512))
    return h3.reshape(b, s, d)
```

```python
import functools
import math

import jax
import jax.numpy as jnp
import numpy as np
from jax import lax
from jax.experimental import pallas as pl
from jax.experimental.pallas import tpu as pltpu

HEAD_DIM = 64
DSA_HEADS = 8
IDX_HEADS = 8
IDX_DIM = 64
TOPK_MAX = 256
SWA_HEADS = 8
SWA_KV_HEADS = 2
WINDOW = 128
MEM_HEADS = 4
ROPE_THETA = 10000.0
LN_EPS = 1e-5
DEPTH = 1
DEEPNORM_ALPHA = (2.0 * DEPTH) ** 0.25

LANES = 128
PACK16 = 16
VMEM_LIMIT = 56 * 1024 * 1024
INT_MIN = -(2 ** 31)
KEY_NEG_INF = (0xFF800000 ^ 0x7FFFFFFF) - (1 << 32)
MASK_VALUE = -1e30
LOG2E = math.log2(math.e)
V_ROWS = HEAD_DIM + PACK16

F32 = jnp.float32
BF16 = jnp.bfloat16
I16 = jnp.int16
NT_DIMS = (((1,), (1,)), ((), ()))


def _cparams(n_axes):
    return pltpu.CompilerParams(
        dimension_semantics=("arbitrary",) * n_axes, vmem_limit_bytes=VMEM_LIMIT)


def _layer_norm(y, g, b):
    mu = jnp.mean(y, axis=-1, keepdims=True)
    d = y - mu
    var = jnp.mean(d * d, axis=-1, keepdims=True)
    return d * lax.rsqrt(var + LN_EPS) * g + b


def _half_mask(shape, low):
    lane = lax.broadcasted_iota(jnp.int32, shape, len(shape) - 1) % LANES
    return (lane < HEAD_DIM) if low else (lane >= HEAD_DIM)


def _move_half(v, src_high, dst_high):
    if src_high != dst_high:
        v = pltpu.roll(v, HEAD_DIM, 1)
    return jnp.where(_half_mask(v.shape, low=not dst_high), v, 0.0)


def _rope_table_kernel(pos_ref, inv_ref, sgn_ref, cos_ref, sin_ref):
    ang = pos_ref[...].astype(F32) * inv_ref[...]
    cos_ref[...] = jnp.cos(ang)
    sin_ref[...] = jnp.sin(ang) * sgn_ref[...]


def _rope_tables(pos_col, inv_row, sgn_row, tm):
    t = pos_col.shape[0]
    return pl.pallas_call(
        _rope_table_kernel,
        out_shape=(jax.ShapeDtypeStruct((t, LANES), F32), jax.ShapeDtypeStruct((t, LANES), F32)),
        grid=(t // tm,),
        in_specs=[pl.BlockSpec((tm, 1), lambda i: (i, 0)),
                  pl.BlockSpec((1, LANES), lambda i: (0, 0)),
                  pl.BlockSpec((1, LANES), lambda i: (0, 0))],
        out_specs=(pl.BlockSpec((tm, LANES), lambda i: (i, 0)),
                   pl.BlockSpec((tm, LANES), lambda i: (i, 0))),
        compiler_params=_cparams(1),
        name="rope_tables",
    )(pos_col, inv_row, sgn_row)


C_DQ, C_IQ, C_SQ, C_SK, C_KK, C_SV, C_MISC, C_END = 0, 512, 1024, 1536, 1664, 1792, 1920, 2048


def _inproj_kernel(x_ref, w_ref, b_ref, cos_ref, sin_ref,
                   qd_ref, qi_ref, qs_ref, sk_ref, kk_ref, sv_ref, misc_ref):
    x = x_ref[...].astype(BF16)
    cos = cos_ref[...]
    sin = sin_ref[...]
    first = (lax.broadcasted_iota(jnp.int32, cos.shape, 1) % HEAD_DIM) < (HEAD_DIM // 2)

    def proj(c0, c1):
        return jnp.dot(x, w_ref[:, c0:c1], preferred_element_type=F32) + b_ref[:, c0:c1]

    def rope(v):
        sw = jnp.where(first, pltpu.roll(v, LANES - HEAD_DIM // 2, 1), pltpu.roll(v, HEAD_DIM // 2, 1))
        return v * cos + sw * sin

    def rope_store(out_ref, c0, c1, scale):
        y = proj(c0, c1)
        for g in range((c1 - c0) // LANES):
            r = rope(y[:, g * LANES:(g + 1) * LANES])
            if scale != 1.0:
                r = r * scale
            out_ref[:, g * LANES:(g + 1) * LANES] = r.astype(out_ref.dtype)

    scale = HEAD_DIM ** -0.5
    rope_store(qd_ref, C_DQ, C_IQ, scale * LOG2E)
    rope_store(qi_ref, C_IQ, C_SQ, 1.0)
    rope_store(qs_ref, C_SQ, C_SK, scale)
    rope_store(sk_ref, C_SK, C_KK, 1.0)
    rope_store(kk_ref, C_KK, C_SV, 1.0)
    sv_ref[...] = proj(C_SV, C_MISC).astype(sv_ref.dtype)
    misc_ref[...] = proj(C_MISC, C_END)


def _in_proj(x2, w, b, cos, sin, tm):
    t, d = x2.shape
    row = lambda i: (i, 0)
    const = lambda i: (0, 0)
    outs = [(512, BF16), (512, BF16), (512, BF16), (128, BF16), (128, BF16), (128, BF16), (128, F32)]
    return pl.pallas_call(
        _inproj_kernel,
        out_shape=tuple(jax.ShapeDtypeStruct((t, n), dt) for n, dt in outs),
        grid=(t // tm,),
        in_specs=[pl.BlockSpec((tm, d), row),
                  pl.BlockSpec((d, C_END), const),
                  pl.BlockSpec((1, C_END), const),
                  pl.BlockSpec((tm, LANES), row),
                  pl.BlockSpec((tm, LANES), row)],
        out_specs=tuple(pl.BlockSpec((tm, n), row) for n, _ in outs),
        compiler_params=_cparams(1),
        name="in_proj",
    )(x2, w, b, cos, sin)


def _to_key(v):
    bits = lax.bitcast_convert_type(v, jnp.int32)
    return bits ^ ((bits >> 31) & 0x7FFFFFFF)


def _tree_sum(parts):
    while len(parts) > 1:
        parts = [parts[k] + parts[k + 1] for k in range(0, len(parts) - 1, 2)] + parts[len(parts) & ~1:]
    return parts[0]


def _dsa_kernel(qi_ref, qd_ref, kk_ref, dvt_ref, iwt_ref, o_ref,
                sc_ref, hi_ref, lo_ref, qip_ref, qdp_ref, r_ref, p_ref, bias_ref, m_ref, acc_ref, ot_ref,
                *, n_sel, row_sub):
    i = pl.program_id(1)
    tq = qi_ref.shape[1]
    tk = kk_ref.shape[2]
    n_tiles = ((i + 1) * tq + tk - 1) // tk
    n_sub = tk // row_sub
    idx_scale = (IDX_DIM ** -0.5) * (IDX_HEADS ** -0.5)

    for h in range(IDX_HEADS):
        grp = slice((h // 2) * LANES, (h // 2 + 1) * LANES)
        qi = qi_ref[0, :, grp].astype(F32)
        qd = qd_ref[0, :, grp].astype(F32)
        qip_ref[h * tq:(h + 1) * tq, :] = _move_half(qi, src_high=(h % 2 == 1), dst_high=True).astype(BF16)
        qdp_ref[h * tq:(h + 1) * tq, :] = _move_half(qd, src_high=(h % 2 == 1), dst_high=False).astype(BF16)

    w_rows = iwt_ref[0] * idx_scale
    t_base = i * tq

    def score_tile(j, carry):
        kt = kk_ref[0, j]
        r_ref[...] = lax.dot_general(kt, qip_ref[...], NT_DIMS, preferred_element_type=F32)
        for rs in range(n_sub):
            rows = slice(rs * row_sub, (rs + 1) * row_sub)
            acc = jnp.zeros((row_sub, tq), F32)
            for h in range(IDX_HEADS):
                acc = acc + jnp.maximum(r_ref[rows, h * tq:(h + 1) * tq], 0.0) * w_rows[h:h + 1, :]
            d0 = (lax.broadcasted_iota(jnp.int32, (row_sub, tq), 0)
                  - lax.broadcasted_iota(jnp.int32, (row_sub, tq), 1))
            causal = d0 <= (t_base - j * tk - rs * row_sub)
            key = jnp.where(causal, _to_key(acc), INT_MIN)
            sc_ref[j, rows, :] = key
            hi_ref[j, rows, :] = (key >> 16).astype(I16)
            lo_ref[j, rows, :] = ((key & 0xFFFF) - 32768).astype(I16)
        return carry

    lax.fori_loop(0, n_tiles, score_tile, 0)

    def search16(ref):
        def count_ge16(cand):
            def body(j, c):
                hit = jnp.where(ref[j] >= cand, jnp.ones((), I16), jnp.zeros((), I16))
                return c + _tree_sum([hit[g * PACK16:(g + 1) * PACK16] for g in range(tk // PACK16)])
            c16 = lax.fori_loop(0, n_tiles, body, jnp.zeros((PACK16, tq), I16))
            return jnp.sum(c16.astype(jnp.int32), axis=0, keepdims=True)

        def bit_step(it, state):
            t_u, t_cnt = state
            cand_u = t_u | lax.shift_left(jnp.int32(1), 15 - it)
            cnt = count_ge16((cand_u - 32768).astype(I16))
            keep = cnt >= n_sel
            return jnp.where(keep, cand_u, t_u), jnp.where(keep, cnt, t_cnt)

        zero = jnp.zeros((1, tq), jnp.int32)
        return lax.fori_loop(0, 16, bit_step, (zero, zero))

    thi_u, cnt_hi = search16(hi_ref)
    thi = thi_u - 32768
    thi16 = thi.astype(I16)

    def fold_hi(j, carry):
        hi = hi_ref[j]
        sat = jnp.where(hi > thi16, jnp.full((), 32767, I16), jnp.full((), -32768, I16))
        lo_ref[j] = jnp.where(hi == thi16, lo_ref[j], sat)
        return carry

    lax.fori_loop(0, n_tiles, fold_hi, 0)
    tlo_u, cnt_lo = search16(lo_ref)
    thr = thi * 65536 + tlo_u
    cnt_ge = jnp.where(tlo_u > 0, cnt_lo, cnt_hi)

    def count_ge(cand):
        def body(j, c):
            hit = jnp.where(sc_ref[j] >= cand, 1, 0)
            return c + _tree_sum([hit[g * 8:(g + 1) * 8] for g in range(tk // 8)])
        c8 = lax.fori_loop(0, n_tiles, body, jnp.zeros((8, tq), jnp.int32))
        return jnp.sum(c8, axis=0, keepdims=True)

    cnt_gt = count_ge(thr + 1)
    need = n_sel - cnt_gt
    has_tie = jnp.max(jnp.where((cnt_ge > n_sel) & (thr > KEY_NEG_INF), 1, 0)) > 0

    @pl.when(has_tie)
    def _():
        n_bits = int(sc_ref.shape[0] * tk - 1).bit_length()

        def count_eq_before(cut):
            def body(j, c):
                kidx = j * tk + lax.broadcasted_iota(jnp.int32, (tk, tq), 0)
                hit = jnp.where((sc_ref[j] == thr) & (kidx < cut), 1, 0)
                return c + _tree_sum([hit[g * 8:(g + 1) * 8] for g in range(tk // 8)])
            c8 = lax.fori_loop(0, n_tiles, body, jnp.zeros((8, tq), jnp.int32))
            return jnp.sum(c8, axis=0, keepdims=True)

        def cut_bit(it, cut):
            bit = lax.shift_left(jnp.int32(1), n_bits - 1 - it)
            cand = cut | bit
            return jnp.where(count_eq_before(cand) < need, cand, cut)

        cut = lax.fori_loop(0, n_bits, cut_bit, jnp.zeros((1, tq), jnp.int32))

        def demote(j, carry):
            kidx = j * tk + lax.broadcasted_iota(jnp.int32, (tk, tq), 0)
            k = sc_ref[j]
            sc_ref[j] = jnp.where((k == thr) & (kidx > cut) & (thr > KEY_NEG_INF), k - 1, k)
            return carry

        lax.fori_loop(0, n_tiles, demote, 0)

    thr_eff = jnp.maximum(thr, KEY_NEG_INF)

    m_ref[...] = jnp.full(m_ref.shape, MASK_VALUE, F32)
    acc_ref[...] = jnp.zeros(acc_ref.shape, F32)

    def logits(j, buf):
        buf[...] = lax.dot_general(kk_ref[0, j], qdp_ref[...], NT_DIMS, preferred_element_type=F32)

    def attend(j, buf):
        vt = dvt_ref[0, j]
        bias_ref[...] = jnp.where(sc_ref[j] >= thr_eff, 0.0, MASK_VALUE)
        for h in range(DSA_HEADS):
            cols = slice(h * tq, (h + 1) * tq)
            m8 = jnp.full((8, tq), MASK_VALUE, F32)
            for rs in range(n_sub):
                rows = slice(rs * row_sub, (rs + 1) * row_sub)
                sm = buf[rows, cols] + bias_ref[rows, :]
                buf[rows, cols] = sm
                m8 = jnp.maximum(m8, jnp.max(sm.reshape(row_sub // 8, 8, tq), axis=0))
            m_old = m_ref[h:h + 1, :]
            m_new = jnp.maximum(m_old, jnp.max(m8, axis=0, keepdims=True))
            alpha = jnp.exp2(m_old - m_new)
            for rs in range(n_sub):
                rows = slice(rs * row_sub, (rs + 1) * row_sub)
                p_ref[rows, cols] = jnp.exp2(buf[rows, cols] - m_new).astype(BF16)
            hd = slice(h * V_ROWS, (h + 1) * V_ROWS)
            acc_ref[hd, :] = alpha * acc_ref[hd, :] + jnp.dot(vt, p_ref[:, cols], preferred_element_type=F32)
            m_ref[h:h + 1, :] = m_new

    def attend_tile(j, carry):
        logits(j, r_ref)
        attend(j, r_ref)
        return carry

    lax.fori_loop(0, n_tiles, attend_tile, 0)

    for h in range(DSA_HEADS):
        den = acc_ref[h * V_ROWS + HEAD_DIM:h * V_ROWS + HEAD_DIM + 1, :]
        ot_ref[h * HEAD_DIM:(h + 1) * HEAD_DIM, :] = acc_ref[h * V_ROWS:h * V_ROWS + HEAD_DIM, :] / den
    o_ref[0] = jnp.transpose(ot_ref[...]).astype(o_ref.dtype)


def _dsa(qi, qd, kk4, dvt4, iwt, tq, n_sel):
    b, s, _ = qi.shape
    nt, tk = kk4.shape[1], kk4.shape[2]
    row_sub = 64 if tk % 64 == 0 else tk
    kern = functools.partial(_dsa_kernel, n_sel=n_sel, row_sub=row_sub)
    return pl.pallas_call(
        kern,
        out_shape=jax.ShapeDtypeStruct((b, s, DSA_HEADS * HEAD_DIM), BF16),
        grid=(b, s // tq),
        in_specs=[pl.BlockSpec((1, tq, 512), lambda bi, i: (bi, i, 0)),
                  pl.BlockSpec((1, tq, 512), lambda bi, i: (bi, i, 0)),
                  pl.BlockSpec((1, nt, tk, LANES), lambda bi, i: (bi, 0, 0, 0)),
                  pl.BlockSpec((1, nt, V_ROWS, tk), lambda bi, i: (bi, 0, 0, 0)),
                  pl.BlockSpec((1, IDX_HEADS, tq), lambda bi, i: (bi, 0, i))],
        out_specs=pl.BlockSpec((1, tq, 512), lambda bi, i: (bi, i, 0)),
        scratch_shapes=[pltpu.VMEM((nt, tk, tq), jnp.int32),
                        pltpu.VMEM((nt, tk, tq), I16),
                        pltpu.VMEM((nt, tk, tq), I16),
                        pltpu.VMEM((IDX_HEADS * tq, LANES), BF16),
                        pltpu.VMEM((DSA_HEADS * tq, LANES), BF16),
                        pltpu.VMEM((tk, IDX_HEADS * tq), F32),
                        pltpu.VMEM((tk, DSA_HEADS * tq), BF16),
                        pltpu.VMEM((tk, tq), F32),
                        pltpu.VMEM((DSA_HEADS, tq), F32),
                        pltpu.VMEM((DSA_HEADS * V_ROWS, tq), F32),
                        pltpu.VMEM((DSA_HEADS * HEAD_DIM, tq), F32)],
        compiler_params=_cparams(2),
        name="dsa",
    )(qi, qd, kk4, dvt4, iwt)


def _swa_kernel(sink_ref, q_ref, kp_ref, kc_ref, vp_ref, vc_ref, o_ref, ot_ref):
    n = pl.program_id(1)
    tq = q_ref.shape[1]
    tb = WINDOW
    group = SWA_HEADS // SWA_KV_HEADS
    kall = jnp.concatenate([kp_ref[0], kc_ref[0]], axis=0)
    vall = jnp.concatenate([vp_ref[0], vc_ref[0]], axis=0)
    vt_all = jnp.transpose(vall.astype(F32)).astype(BF16)
    ones = jnp.ones((V_ROWS - HEAD_DIM, 2 * tb), BF16)
    shape = (2 * tb, group * tb)
    k_i = lax.broadcasted_iota(jnp.int32, shape, 0)
    q_i = lax.broadcasted_iota(jnp.int32, shape, 1) & (tb - 1)
    diff = q_i + tb - k_i
    inband = (diff >= 0) & (diff < WINDOW)
    for u in range(tq // tb):
        kband = kall[u * tb:(u + 2) * tb]
        valid = (inband & ((k_i >= tb) | (n > 0))) if u == 0 else inband
        for g in range(SWA_KV_HEADS):
            heads = range(g * group, (g + 1) * group)
            qs = []
            for h in heads:
                qg = q_ref[0, u * tb:(u + 1) * tb, (h // 2) * LANES:(h // 2 + 1) * LANES].astype(F32)
                qs.append(_move_half(qg, src_high=(h % 2 == 1), dst_high=(g == 1)).astype(BF16))
            s = lax.dot_general(kband, jnp.concatenate(qs, axis=0), NT_DIMS,
                                preferred_element_type=F32)
            s = jnp.where(valid, s, MASK_VALUE)
            sink = jnp.concatenate([jnp.full((1, tb), sink_ref[h], F32) for h in heads], axis=1)
            m = jnp.maximum(jnp.max(s, axis=0, keepdims=True), sink)
            p = jnp.exp(s - m).astype(BF16)
            vt = jnp.concatenate([vt_all[g * HEAD_DIM:(g + 1) * HEAD_DIM, u * tb:(u + 2) * tb], ones], axis=0)
            acc = jnp.dot(vt, p, preferred_element_type=F32)
            den = acc[HEAD_DIM:HEAD_DIM + 1, :] + jnp.exp(sink - m)
            on = acc[:HEAD_DIM, :] / den
            for hh, h in enumerate(heads):
                ot_ref[h * HEAD_DIM:(h + 1) * HEAD_DIM, :] = on[:, hh * tb:(hh + 1) * tb]
        o_ref[0, u * tb:(u + 1) * tb, :] = jnp.transpose(ot_ref[...]).astype(o_ref.dtype)


def _swa(sinks, qs, sk, sv, tq):
    b, s, _ = qs.shape
    ratio = tq // WINDOW
    cur = lambda bi, n: (bi, n, 0)
    prev = lambda bi, n: (bi, jnp.maximum(n * ratio - 1, 0), 0)
    return pl.pallas_call(
        _swa_kernel,
        out_shape=jax.ShapeDtypeStruct((b, s, SWA_HEADS * HEAD_DIM), BF16),
        grid=(b, s // tq),
        in_specs=[pl.BlockSpec(memory_space=pltpu.SMEM),
                  pl.BlockSpec((1, tq, 512), cur),
                  pl.BlockSpec((1, WINDOW, LANES), prev),
                  pl.BlockSpec((1, tq, LANES), cur),
                  pl.BlockSpec((1, WINDOW, LANES), prev),
                  pl.BlockSpec((1, tq, LANES), cur)],
        out_specs=pl.BlockSpec((1, tq, 512), cur),
        scratch_shapes=[pltpu.VMEM((SWA_HEADS * HEAD_DIM, WINDOW), F32)],
        compiler_params=_cparams(2),
        name="swa",
    )(sinks, qs, sk, sk, sv, sv)


def _outproj_kernel(x_ref, od_ref, os_ref, w_ref, g_ref, b_ref, h_ref):
    half = od_ref.shape[1]
    mix = (jnp.dot(od_ref[...], w_ref[:half, :], preferred_element_type=F32)
           + jnp.dot(os_ref[...], w_ref[half:, :], preferred_element_type=F32))
    h_ref[...] = _layer_norm(DEEPNORM_ALPHA * x_ref[...] + mix, g_ref[...], b_ref[...])


def _out_proj_ln(x2, o_dsa, o_swa, w_o, g, bta, tm):
    t, d = x2.shape
    row = lambda i: (i, 0)
    const = lambda i: (0, 0)
    return pl.pallas_call(
        _outproj_kernel,
        out_shape=jax.ShapeDtypeStruct((t, d), F32),
        grid=(t // tm,),
        in_specs=[pl.BlockSpec((tm, d), row),
                  pl.BlockSpec((tm, o_dsa.shape[1]), row),
                  pl.BlockSpec((tm, o_swa.shape[1]), row),
                  pl.BlockSpec(w_o.shape, const),
                  pl.BlockSpec((1, d), const),
                  pl.BlockSpec((1, d), const)],
        out_specs=pl.BlockSpec((tm, d), row),
        compiler_params=_cparams(1),
        name="out_proj_ln",
    )(x2, o_dsa, o_swa, w_o, g, bta)


def _memkv_kernel(mem_ref, wk_ref, wv_ref, k_ref, v_ref):
    m = mem_ref[...].astype(BF16)
    k_ref[...] = jnp.dot(m, wk_ref[...], preferred_element_type=F32).astype(k_ref.dtype)
    v_ref[...] = jnp.dot(m, wv_ref[...], preferred_element_type=F32).astype(v_ref.dtype)


def _mem_kv(mem2, wk, wv, tm):
    t, d = mem2.shape
    row = lambda i: (i, 0)
    const = lambda i: (0, 0)
    return pl.pallas_call(
        _memkv_kernel,
        out_shape=(jax.ShapeDtypeStruct((t, d), BF16), jax.ShapeDtypeStruct((t, d), BF16)),
        grid=(t // tm,),
        in_specs=[pl.BlockSpec((tm, d), row), pl.BlockSpec(wk.shape, const), pl.BlockSpec(wv.shape, const)],
        out_specs=(pl.BlockSpec((tm, d), row), pl.BlockSpec((tm, d), row)),
        compiler_params=_cparams(1),
        name="mem_kv",
    )(mem2, wk, wv)


def _cross_kernel(h_ref, wq_ref, km_ref, vm_ref, wo_ref, g_ref, b_ref, o_ref):
    hf = h_ref[0]
    d = hf.shape[1]
    hd = d // MEM_HEADS
    q = (jnp.dot(hf.astype(BF16), wq_ref[...], preferred_element_type=F32) * (hd ** -0.5)).astype(BF16)
    outs = []
    for hh in range(MEM_HEADS):
        cols = slice(hh * hd, (hh + 1) * hd)
        s = lax.dot_general(q[:, cols], km_ref[0, :, cols], NT_DIMS, preferred_element_type=F32)
        m = jnp.max(s, axis=-1, keepdims=True)
        p = jnp.exp(s - m)
        den = jnp.sum(p, axis=-1, keepdims=True)
        o = jnp.dot(p.astype(BF16), vm_ref[0, :, cols], preferred_element_type=F32) / den
        outs.append(o.astype(BF16))
    o_all = jnp.concatenate(outs, axis=-1)
    c = jnp.dot(o_all, wo_ref[...], preferred_element_type=F32)
    o_ref[0] = _layer_norm(DEEPNORM_ALPHA * hf + c, g_ref[...], b_ref[...])


def _cross_ln(h3, wq, km, vm, wo, g, bta, tm):
    b, s, d = h3.shape
    m = km.shape[1]
    row = lambda bi, i: (bi, i, 0)
    per_b = lambda bi, i: (bi, 0, 0)
    const = lambda bi, i: (0, 0)
    return pl.pallas_call(
        _cross_kernel,
        out_shape=jax.ShapeDtypeStruct((b, s, d), F32),
        grid=(b, s // tm),
        in_specs=[pl.BlockSpec((1, tm, d), row),
                  pl.BlockSpec(wq.shape, const),
                  pl.BlockSpec((1, m, d), per_b),
                  pl.BlockSpec((1, m, d), per_b),
                  pl.BlockSpec(wo.shape, const),
                  pl.BlockSpec((1, d), const),
                  pl.BlockSpec((1, d), const)],
        out_specs=pl.BlockSpec((1, tm, d), row),
        compiler_params=_cparams(2),
        name="cross_ln",
    )(h3, wq, km, vm, wo, g, bta)


def _mlp_kernel(h_ref, wu_ref, wd_ref, g_ref, b_ref, o_ref, *, f_chunk):
    hf = h_ref[...]
    hb = hf.astype(BF16)
    acc = DEEPNORM_ALPHA * hf
    for c0 in range(0, wu_ref.shape[1], f_chunk):
        u = jnp.maximum(jnp.dot(hb, wu_ref[:, c0:c0 + f_chunk], preferred_element_type=F32), 0.0)
        acc = acc + jnp.dot((u * u).astype(BF16), wd_ref[c0:c0 + f_chunk, :], preferred_element_type=F32)
    o_ref[...] = _layer_norm(acc, g_ref[...], b_ref[...])


def _mlp_ln(h2, w_up, w_down, g, bta, tm):
    t, d = h2.shape
    row = lambda i: (i, 0)
    const = lambda i: (0, 0)
    once = pl.Buffered(1)
    return pl.pallas_call(
        functools.partial(_mlp_kernel, f_chunk=1024),
        out_shape=jax.ShapeDtypeStruct((t, d), F32),
        grid=(t // tm,),
        in_specs=[pl.BlockSpec((tm, d), row),
                  pl.BlockSpec(w_up.shape, const, pipeline_mode=once),
                  pl.BlockSpec(w_down.shape, const, pipeline_mode=once),
                  pl.BlockSpec((1, d), const),
                  pl.BlockSpec((1, d), const)],
        out_specs=pl.BlockSpec((tm, d), row),
        compiler_params=_cparams(1),
        name="mlp_ln",
    )(h2, w_up, w_down, g, bta)


def _tile(n, pref):
    while n % pref:
        pref //= 2
    return pref


def kernel(x, mem, positions, w_in, b_in, swa_sinks, w_o, ln1_g, ln1_b, wq_mem, wk_mem, wv_mem, wo_mem,
           ln2_g, ln2_b, w_up, w_down, ln3_g, ln3_b):
    b, s, d = x.shape
    t = b * s
    assert w_in.shape[0] == DEPTH and s % LANES == 0
    n_sel = min(TOPK_MAX, s // 4)

    inv = ROPE_THETA ** (-jnp.arange(0, HEAD_DIM, 2, dtype=F32) / HEAD_DIM)
    inv_row = jnp.tile(inv, LANES // inv.shape[0])[None, :]
    sgn_row = jnp.where((jnp.arange(LANES) % HEAD_DIM) < HEAD_DIM // 2, -1.0, 1.0).astype(F32)[None, :]
    sizes = (DSA_HEADS * HEAD_DIM, HEAD_DIM, HEAD_DIM, IDX_HEADS * IDX_DIM, IDX_DIM, IDX_HEADS,
             SWA_HEADS * HEAD_DIM, SWA_KV_HEADS * HEAD_DIM, SWA_KV_HEADS * HEAD_DIM)
    off = np.concatenate([[0], np.cumsum(sizes)])
    order = (0, 3, 6, 7, 1, 4, 8, 2, 5)
    pad = C_END - int(off[-1])
    w_p = jnp.concatenate([w_in[0][:, off[k]:off[k + 1]] for k in order]
                          + [jnp.zeros((d, pad), w_in.dtype)], axis=1).astype(BF16)
    b_p = jnp.concatenate([b_in[0][off[k]:off[k + 1]] for k in order] + [jnp.zeros((pad,), b_in.dtype)])[None, :]

    x2 = x.reshape(t, d)
    cos, sin = _rope_tables(positions.reshape(t, 1), inv_row, sgn_row, _tile(t, 2048))
    qd, qi, qs, sk, kk, sv, misc = _in_proj(x2, w_p, b_p, cos, sin, _tile(t, 512))

    tq = _tile(s, 256)
    tk = _tile(s, 2 * tq)
    nt = s // tk
    dvt = jnp.swapaxes(misc[:, :HEAD_DIM].astype(BF16).reshape(b, nt, tk, HEAD_DIM), 2, 3)
    dvt4 = jnp.concatenate([dvt, jnp.ones((b, nt, V_ROWS - HEAD_DIM, tk), BF16)], axis=2)
    iwt = jnp.swapaxes(misc[:, HEAD_DIM:HEAD_DIM + IDX_HEADS].reshape(b, s, IDX_HEADS), 1, 2)
    o_dsa = _dsa(qi.reshape(b, s, -1), qd.reshape(b, s, -1), kk.reshape(b, nt, tk, LANES), dvt4, iwt, tq, n_sel)

    o_swa = _swa(swa_sinks[0], qs.reshape(b, s, -1), sk.reshape(b, s, LANES), sv.reshape(b, s, LANES),
                 _tile(s, 512))

    h1 = _out_proj_ln(x2, o_dsa.reshape(t, -1), o_swa.reshape(t, -1), w_o[0].astype(BF16),
                      ln1_g[0][None, :], ln1_b[0][None, :], _tile(t, 1024))

    mt = mem.shape[1]
    km, vm = _mem_kv(mem.reshape(b * mt, d), wk_mem[0].astype(BF16), wv_mem[0].astype(BF16), _tile(b * mt, 256))
    h2 = _cross_ln(h1.reshape(b, s, d), wq_mem[0].astype(BF16), km.reshape(b, mt, d), vm.reshape(b, mt, d),
                   wo_mem[0].astype(BF16), ln2_g[0][None, :], ln2_b[0][None, :], _tile(s, 512))

    h3 = _mlp_ln(h2.reshape(t, d), w_up[0].astype(BF16), w_down[0].astype(BF16),
                 ln3_g[0][None, :], ln3_b[0][None, :], _tile(t, 512))
    return h3.reshape(b, s, d)
```

```python
import functools
import math

import jax
import jax.numpy as jnp
import numpy as np
from jax import lax
from jax.experimental import pallas as pl
from jax.experimental.pallas import tpu as pltpu

HEAD_DIM = 64
DSA_HEADS = 8
IDX_HEADS = 8
IDX_DIM = 64
TOPK_MAX = 256
SWA_HEADS = 8
SWA_KV_HEADS = 2
WINDOW = 128
MEM_HEADS = 4
ROPE_THETA = 10000.0
LN_EPS = 1e-5
DEPTH = 1
DEEPNORM_ALPHA = (2.0 * DEPTH) ** 0.25

LANES = 128
PACK16 = 16
VMEM_LIMIT = 56 * 1024 * 1024
INT_MIN = -(2 ** 31)
KEY_NEG_INF = (0xFF800000 ^ 0x7FFFFFFF) - (1 << 32)
MASK_VALUE = -1e30
TIE_STEPS = 8
LOG2E = math.log2(math.e)
V_ROWS = HEAD_DIM + PACK16

F32 = jnp.float32
BF16 = jnp.bfloat16
I16 = jnp.int16
NT_DIMS = (((1,), (1,)), ((), ()))


def _cparams(n_axes):
    return pltpu.CompilerParams(
        dimension_semantics=("arbitrary",) * n_axes, vmem_limit_bytes=VMEM_LIMIT)


def _layer_norm(y, g, b):
    mu = jnp.mean(y, axis=-1, keepdims=True)
    d = y - mu
    var = jnp.mean(d * d, axis=-1, keepdims=True)
    return d * lax.rsqrt(var + LN_EPS) * g + b


def _half_mask(shape, low):
    lane = lax.broadcasted_iota(jnp.int32, shape, len(shape) - 1) % LANES
    return (lane < HEAD_DIM) if low else (lane >= HEAD_DIM)


def _move_half(v, src_high, dst_high):
    if src_high != dst_high:
        v = pltpu.roll(v, HEAD_DIM, 1)
    return jnp.where(_half_mask(v.shape, low=not dst_high), v, 0.0)


def _rope_table_kernel(pos_ref, inv_ref, sgn_ref, cos_ref, sin_ref):
    ang = pos_ref[...].astype(F32) * inv_ref[...]
    cos_ref[...] = jnp.cos(ang)
    sin_ref[...] = jnp.sin(ang) * sgn_ref[...]


def _rope_tables(pos_col, inv_row, sgn_row, tm):
    t = pos_col.shape[0]
    return pl.pallas_call(
        _rope_table_kernel,
        out_shape=(jax.ShapeDtypeStruct((t, LANES), F32), jax.ShapeDtypeStruct((t, LANES), F32)),
        grid=(t // tm,),
        in_specs=[pl.BlockSpec((tm, 1), lambda i: (i, 0)),
                  pl.BlockSpec((1, LANES), lambda i: (0, 0)),
                  pl.BlockSpec((1, LANES), lambda i: (0, 0))],
        out_specs=(pl.BlockSpec((tm, LANES), lambda i: (i, 0)),
                   pl.BlockSpec((tm, LANES), lambda i: (i, 0))),
        compiler_params=_cparams(1),
        name="rope_tables",
    )(pos_col, inv_row, sgn_row)


C_DQ, C_IQ, C_SQ, C_SK, C_KK, C_SV, C_MISC, C_END = 0, 512, 1024, 1536, 1664, 1792, 1920, 2048


def _inproj_kernel(x_ref, w_ref, b_ref, cos_ref, sin_ref,
                   qd_ref, qi_ref, qs_ref, sk_ref, kk_ref, sv_ref, dvt_ref, iwt_ref):
    x = x_ref[...].astype(BF16)
    cos = cos_ref[...]
    sin = sin_ref[...]
    first = (lax.broadcasted_iota(jnp.int32, cos.shape, 1) % HEAD_DIM) < (HEAD_DIM // 2)

    def proj(c0, c1):
        return jnp.dot(x, w_ref[:, c0:c1], preferred_element_type=F32) + b_ref[:, c0:c1]

    def rope(v):
        sw = jnp.where(first, pltpu.roll(v, LANES - HEAD_DIM // 2, 1), pltpu.roll(v, HEAD_DIM // 2, 1))
        return v * cos + sw * sin

    def rope_store(out_ref, c0, c1, scale):
        y = proj(c0, c1)
        for g in range((c1 - c0) // LANES):
            r = rope(y[:, g * LANES:(g + 1) * LANES])
            if scale != 1.0:
                r = r * scale
            out_ref[:, g * LANES:(g + 1) * LANES] = r.astype(out_ref.dtype)

    scale = HEAD_DIM ** -0.5
    rope_store(qd_ref, C_DQ, C_IQ, scale * LOG2E)
    rope_store(qi_ref, C_IQ, C_SQ, 1.0)
    rope_store(qs_ref, C_SQ, C_SK, scale)
    rope_store(sk_ref, C_SK, C_KK, 1.0)
    rope_store(kk_ref, C_KK, C_SV, 1.0)
    sv_ref[...] = proj(C_SV, C_MISC).astype(sv_ref.dtype)
    mt = jnp.transpose(proj(C_MISC, C_END))
    dvt_ref[0, :HEAD_DIM, :] = mt[:HEAD_DIM].astype(dvt_ref.dtype)
    dvt_ref[0, HEAD_DIM:, :] = jnp.ones((V_ROWS - HEAD_DIM, mt.shape[1]), dvt_ref.dtype)
    iwt_ref[0] = mt[HEAD_DIM:HEAD_DIM + IDX_HEADS]


def _in_proj(x2, w, b, cos, sin, tm):
    t, d = x2.shape
    row = lambda i: (i, 0)
    const = lambda i: (0, 0)
    tile3 = lambda i: (i, 0, 0)
    outs = [(512, BF16), (512, BF16), (512, BF16), (128, BF16), (128, BF16), (128, BF16)]
    return pl.pallas_call(
        _inproj_kernel,
        out_shape=tuple(jax.ShapeDtypeStruct((t, n), dt) for n, dt in outs)
        + (jax.ShapeDtypeStruct((t // tm, V_ROWS, tm), BF16), jax.ShapeDtypeStruct((t // tm, IDX_HEADS, tm), F32)),
        grid=(t // tm,),
        in_specs=[pl.BlockSpec((tm, d), row),
                  pl.BlockSpec((d, C_END), const),
                  pl.BlockSpec((1, C_END), const),
                  pl.BlockSpec((tm, LANES), row),
                  pl.BlockSpec((tm, LANES), row)],
        out_specs=tuple(pl.BlockSpec((tm, n), row) for n, _ in outs)
        + (pl.BlockSpec((1, V_ROWS, tm), tile3), pl.BlockSpec((1, IDX_HEADS, tm), tile3)),
        compiler_params=_cparams(1),
        name="in_proj",
    )(x2, w, b, cos, sin)


def _to_key(v):
    bits = lax.bitcast_convert_type(v, jnp.int32)
    return bits ^ ((bits >> 31) & 0x7FFFFFFF)


def _tree_sum(parts):
    while len(parts) > 1:
        parts = [parts[k] + parts[k + 1] for k in range(0, len(parts) - 1, 2)] + parts[len(parts) & ~1:]
    return parts[0]


def _tree_max(parts):
    while len(parts) > 1:
        parts = [jnp.maximum(parts[k], parts[k + 1]) for k in range(0, len(parts) - 1, 2)] + parts[len(parts) & ~1:]
    return parts[0]


def _dsa_kernel(qi_ref, qd_ref, kk_ref, dvt_ref, iwt_ref, o_ref,
                sc_ref, hi_ref, lo_ref, qip_ref, qdp_ref, r_ref, p_ref, bias_ref, m_ref, acc_ref, ot_ref,
                *, n_sel, row_sub):
    i = pl.program_id(1)
    tq = qi_ref.shape[1]
    tk = kk_ref.shape[2]
    n_tiles = ((i + 1) * tq + tk - 1) // tk
    n_sub = tk // row_sub
    idx_scale = (IDX_DIM ** -0.5) * (IDX_HEADS ** -0.5)

    for h in range(IDX_HEADS):
        grp = slice((h // 2) * LANES, (h // 2 + 1) * LANES)
        qi = qi_ref[0, :, grp].astype(F32)
        qd = qd_ref[0, :, grp].astype(F32)
        qip_ref[h * tq:(h + 1) * tq, :] = _move_half(qi, src_high=(h % 2 == 1), dst_high=True).astype(BF16)
        qdp_ref[h * tq:(h + 1) * tq, :] = _move_half(qd, src_high=(h % 2 == 1), dst_high=False).astype(BF16)

    w_rows = iwt_ref[0] * idx_scale
    t_base = i * tq

    def score_tile(j, carry):
        kt = kk_ref[0, j]
        r_ref[...] = lax.dot_general(kt, qip_ref[...], NT_DIMS, preferred_element_type=F32)
        for rs in range(n_sub):
            rows = slice(rs * row_sub, (rs + 1) * row_sub)
            acc = jnp.zeros((row_sub, tq), F32)
            for h in range(IDX_HEADS):
                acc = acc + jnp.maximum(r_ref[rows, h * tq:(h + 1) * tq], 0.0) * w_rows[h:h + 1, :]
            d0 = (lax.broadcasted_iota(jnp.int32, (row_sub, tq), 0)
                  - lax.broadcasted_iota(jnp.int32, (row_sub, tq), 1))
            causal = d0 <= (t_base - j * tk - rs * row_sub)
            key = jnp.where(causal, _to_key(acc), INT_MIN)
            sc_ref[j, rows, :] = key
            hi_ref[j, rows, :] = (key >> 16).astype(I16)
            lo_ref[j, rows, :] = ((key & 0xFFFF) - 32768).astype(I16)
        return carry

    lax.fori_loop(0, n_tiles, score_tile, 0)

    def count_ge16(ref, cand_u):
        cand = (cand_u - 32768).astype(I16)

        def body(j, c):
            hit = jnp.where(ref[j] >= cand, jnp.ones((), I16), jnp.zeros((), I16))
            return c + _tree_sum([hit[g * PACK16:(g + 1) * PACK16] for g in range(tk // PACK16)])
        c16 = lax.fori_loop(0, n_tiles, body, jnp.zeros((PACK16, tq), I16))
        return jnp.sum(c16.astype(jnp.int32), axis=0, keepdims=True)

    def bisect16(ref, lo, hi, lo_cnt, steps):
        def step(_, state):
            lo, hi, lo_cnt = state
            mid = lo + ((hi - lo + 1) >> 1)
            cnt = count_ge16(ref, mid)
            keep = (cnt >= n_sel) & (hi > lo)
            return jnp.where(keep, mid, lo), jnp.where(keep, hi, jnp.maximum(mid - 1, lo)), jnp.where(keep, cnt, lo_cnt)
        lo, _, lo_cnt = lax.fori_loop(0, steps, step, (lo, hi, lo_cnt))
        return lo, lo_cnt

    zero = jnp.zeros((1, tq), jnp.int32)
    full_hi = jnp.full((1, tq), 65535, jnp.int32)
    thi_u, cnt_hi = bisect16(hi_ref, zero, full_hi, zero, 16)
    thi = thi_u - 32768
    thi16 = thi.astype(I16)

    def fold_hi(j, carry):
        hi = hi_ref[j]
        sat = jnp.where(hi > thi16, jnp.full((), 32767, I16), jnp.full((), -32768, I16))
        lo_ref[j] = jnp.where(hi == thi16, lo_ref[j], sat)
        return carry

    lax.fori_loop(0, n_tiles, fold_hi, 0)
    tlo_u, cnt_lo = bisect16(lo_ref, zero, full_hi, zero, 16)
    thr = thi * 65536 + tlo_u
    cnt_ge = jnp.where(tlo_u > 0, cnt_lo, cnt_hi)

    excess = jnp.where(thr > KEY_NEG_INF, cnt_ge - n_sel, 0)
    max_excess = jnp.max(excess)

    @pl.when((max_excess > 0) & (max_excess <= TIE_STEPS))
    def _():
        def drop_one(it, carry):
            live = excess > it

            def scan(j, mx):
                kidx = j * tk + lax.broadcasted_iota(jnp.int32, (tk, tq), 0)
                cand = jnp.where(sc_ref[j] == thr, kidx, -1)
                return jnp.maximum(mx, _tree_max([cand[g * 8:(g + 1) * 8] for g in range(tk // 8)]))
            top = jnp.max(lax.fori_loop(0, n_tiles, scan, jnp.full((8, tq), -1, jnp.int32)),
                          axis=0, keepdims=True)

            def drop(j, c):
                kidx = j * tk + lax.broadcasted_iota(jnp.int32, (tk, tq), 0)
                k = sc_ref[j]
                sc_ref[j] = jnp.where((k == thr) & (kidx == top) & live, k - 1, k)
                return c
            lax.fori_loop(0, n_tiles, drop, 0)
            return carry

        lax.fori_loop(0, max_excess, drop_one, 0)

    @pl.when(max_excess > TIE_STEPS)
    def _():
        n_bits = int(sc_ref.shape[0] * tk - 1).bit_length()
        row16 = lax.broadcasted_iota(jnp.int32, (tk, tq), 0).astype(I16)
        tlo16 = (tlo_u - 32768).astype(I16)
        none16 = jnp.full((), -1, I16)

        def mark(j, carry):
            lo_ref[j] = jnp.where(hi_ref[j] == thi16, jnp.where(lo_ref[j] == tlo16, row16, none16), none16)
            return carry

        lax.fori_loop(0, n_tiles, mark, 0)

        def count_tied_from(cand):
            def body(j, c):
                cj = jnp.clip(cand - j * tk, 0, 32767).astype(I16)
                hit = jnp.where(lo_ref[j] >= cj, jnp.ones((), I16), jnp.zeros((), I16))
                return c + _tree_sum([hit[g * PACK16:(g + 1) * PACK16] for g in range(tk // PACK16)])
            c16 = lax.fori_loop(0, n_tiles, body, jnp.zeros((PACK16, tq), I16))
            return jnp.sum(c16.astype(jnp.int32), axis=0, keepdims=True)

        def cut_bit(it, cut):
            cand = cut | lax.shift_left(jnp.int32(1), n_bits - 1 - it)
            return jnp.where(count_tied_from(cand) >= excess, cand, cut)

        first_drop = lax.fori_loop(0, n_bits, cut_bit, jnp.zeros((1, tq), jnp.int32))

        def demote(j, carry):
            kidx = j * tk + lax.broadcasted_iota(jnp.int32, (tk, tq), 0)
            k = sc_ref[j]
            sc_ref[j] = jnp.where((k == thr) & (kidx >= first_drop) & (excess > 0), k - 1, k)
            return carry

        lax.fori_loop(0, n_tiles, demote, 0)

    thr_eff = jnp.maximum(thr, KEY_NEG_INF)

    m_ref[...] = jnp.full(m_ref.shape, MASK_VALUE, F32)
    acc_ref[...] = jnp.zeros(acc_ref.shape, F32)

    def logits(j, buf):
        buf[...] = lax.dot_general(kk_ref[0, j], qdp_ref[...], NT_DIMS, preferred_element_type=F32)

    def attend(j, buf):
        vt = dvt_ref[0, j]
        m8 = [jnp.full((8, tq), MASK_VALUE, F32) for _ in range(DSA_HEADS)]
        for rs in range(n_sub):
            rows = slice(rs * row_sub, (rs + 1) * row_sub)
            bias = jnp.where(sc_ref[j, rows, :] >= thr_eff, 0.0, MASK_VALUE)
            for h in range(DSA_HEADS):
                cols = slice(h * tq, (h + 1) * tq)
                sm = buf[rows, cols] + bias
                buf[rows, cols] = sm
                m8[h] = jnp.maximum(m8[h], _tree_max([sm[g * 8:(g + 1) * 8] for g in range(row_sub // 8)]))
        for h in range(DSA_HEADS):
            cols = slice(h * tq, (h + 1) * tq)
            m_old = m_ref[h:h + 1, :]
            m_new = jnp.maximum(m_old, jnp.max(m8[h], axis=0, keepdims=True))
            alpha = jnp.exp2(m_old - m_new)
            for rs in range(n_sub):
                rows = slice(rs * row_sub, (rs + 1) * row_sub)
                p_ref[rows, cols] = jnp.exp2(buf[rows, cols] - m_new).astype(BF16)
            hd = slice(h * V_ROWS, (h + 1) * V_ROWS)
            acc_ref[hd, :] = alpha * acc_ref[hd, :] + jnp.dot(vt, p_ref[:, cols], preferred_element_type=F32)
            m_ref[h:h + 1, :] = m_new

    def attend_tile(j, carry):
        logits(j, r_ref)
        attend(j, r_ref)
        return carry

    lax.fori_loop(0, n_tiles, attend_tile, 0)

    for h in range(DSA_HEADS):
        den = acc_ref[h * V_ROWS + HEAD_DIM:h * V_ROWS + HEAD_DIM + 1, :]
        ot_ref[h * HEAD_DIM:(h + 1) * HEAD_DIM, :] = acc_ref[h * V_ROWS:h * V_ROWS + HEAD_DIM, :] / den
    o_ref[0] = jnp.transpose(ot_ref[...]).astype(o_ref.dtype)


def _dsa(qi, qd, kk4, dvt4, iwt, tq, n_sel):
    b, s, _ = qi.shape
    nt, tk = kk4.shape[1], kk4.shape[2]
    row_sub = 64 if tk % 64 == 0 else tk
    kern = functools.partial(_dsa_kernel, n_sel=n_sel, row_sub=row_sub)
    return pl.pallas_call(
        kern,
        out_shape=jax.ShapeDtypeStruct((b, s, DSA_HEADS * HEAD_DIM), BF16),
        grid=(b, s // tq),
        in_specs=[pl.BlockSpec((1, tq, 512), lambda bi, i: (bi, i, 0)),
                  pl.BlockSpec((1, tq, 512), lambda bi, i: (bi, i, 0)),
                  pl.BlockSpec((1, nt, tk, LANES), lambda bi, i: (bi, 0, 0, 0)),
                  pl.BlockSpec((1, nt, V_ROWS, tk), lambda bi, i: (bi, 0, 0, 0)),
                  pl.BlockSpec((1, IDX_HEADS, tq), lambda bi, i: (bi, 0, i))],
        out_specs=pl.BlockSpec((1, tq, 512), lambda bi, i: (bi, i, 0)),
        scratch_shapes=[pltpu.VMEM((nt, tk, tq), jnp.int32),
                        pltpu.VMEM((nt, tk, tq), I16),
                        pltpu.VMEM((nt, tk, tq), I16),
                        pltpu.VMEM((IDX_HEADS * tq, LANES), BF16),
                        pltpu.VMEM((DSA_HEADS * tq, LANES), BF16),
                        pltpu.VMEM((tk, IDX_HEADS * tq), F32),
                        pltpu.VMEM((tk, DSA_HEADS * tq), BF16),
                        pltpu.VMEM((tk, tq), F32),
                        pltpu.VMEM((DSA_HEADS, tq), F32),
                        pltpu.VMEM((DSA_HEADS * V_ROWS, tq), F32),
                        pltpu.VMEM((DSA_HEADS * HEAD_DIM, tq), F32)],
        compiler_params=_cparams(2),
        name="dsa",
    )(qi, qd, kk4, dvt4, iwt)


def _swa_kernel(sink_ref, q_ref, kp_ref, kc_ref, vp_ref, vc_ref, o_ref, ot_ref):
    n = pl.program_id(1)
    tq = q_ref.shape[1]
    tb = WINDOW
    group = SWA_HEADS // SWA_KV_HEADS
    kall = jnp.concatenate([kp_ref[0], kc_ref[0]], axis=0)
    vall = jnp.concatenate([vp_ref[0], vc_ref[0]], axis=0)
    vt_all = jnp.transpose(vall.astype(F32)).astype(BF16)
    ones = jnp.ones((V_ROWS - HEAD_DIM, 2 * tb), BF16)
    shape = (2 * tb, group * tb)
    k_i = lax.broadcasted_iota(jnp.int32, shape, 0)
    q_i = lax.broadcasted_iota(jnp.int32, shape, 1) & (tb - 1)
    diff = q_i + tb - k_i
    inband = (diff >= 0) & (diff < WINDOW)
    for u in range(tq // tb):
        kband = kall[u * tb:(u + 2) * tb]
        valid = (inband & ((k_i >= tb) | (n > 0))) if u == 0 else inband
        for g in range(SWA_KV_HEADS):
            heads = range(g * group, (g + 1) * group)
            qs = []
            for h in heads:
                qg = q_ref[0, u * tb:(u + 1) * tb, (h // 2) * LANES:(h // 2 + 1) * LANES].astype(F32)
                qs.append(_move_half(qg, src_high=(h % 2 == 1), dst_high=(g == 1)).astype(BF16))
            s = lax.dot_general(kband, jnp.concatenate(qs, axis=0), NT_DIMS,
                                preferred_element_type=F32)
            s = jnp.where(valid, s, MASK_VALUE)
            sink = jnp.concatenate([jnp.full((1, tb), sink_ref[h], F32) for h in heads], axis=1)
            m = jnp.maximum(jnp.max(s, axis=0, keepdims=True), sink)
            p = jnp.exp(s - m).astype(BF16)
            vt = jnp.concatenate([vt_all[g * HEAD_DIM:(g + 1) * HEAD_DIM, u * tb:(u + 2) * tb], ones], axis=0)
            acc = jnp.dot(vt, p, preferred_element_type=F32)
            den = acc[HEAD_DIM:HEAD_DIM + 1, :] + jnp.exp(sink - m)
            on = acc[:HEAD_DIM, :] / den
            for hh, h in enumerate(heads):
                ot_ref[h * HEAD_DIM:(h + 1) * HEAD_DIM, :] = on[:, hh * tb:(hh + 1) * tb]
        o_ref[0, u * tb:(u + 1) * tb, :] = jnp.transpose(ot_ref[...]).astype(o_ref.dtype)


def _swa(sinks, qs, sk, sv, tq):
    b, s, _ = qs.shape
    ratio = tq // WINDOW
    cur = lambda bi, n: (bi, n, 0)
    prev = lambda bi, n: (bi, jnp.maximum(n * ratio - 1, 0), 0)
    return pl.pallas_call(
        _swa_kernel,
        out_shape=jax.ShapeDtypeStruct((b, s, SWA_HEADS * HEAD_DIM), BF16),
        grid=(b, s // tq),
        in_specs=[pl.BlockSpec(memory_space=pltpu.SMEM),
                  pl.BlockSpec((1, tq, 512), cur),
                  pl.BlockSpec((1, WINDOW, LANES), prev),
                  pl.BlockSpec((1, tq, LANES), cur),
                  pl.BlockSpec((1, WINDOW, LANES), prev),
                  pl.BlockSpec((1, tq, LANES), cur)],
        out_specs=pl.BlockSpec((1, tq, 512), cur),
        scratch_shapes=[pltpu.VMEM((SWA_HEADS * HEAD_DIM, WINDOW), F32)],
        compiler_params=_cparams(2),
        name="swa",
    )(sinks, qs, sk, sk, sv, sv)


def _outproj_kernel(x_ref, od_ref, os_ref, w_ref, g_ref, b_ref, h_ref):
    half = od_ref.shape[1]
    mix = (jnp.dot(od_ref[...], w_ref[:half, :], preferred_element_type=F32)
           + jnp.dot(os_ref[...], w_ref[half:, :], preferred_element_type=F32))
    h_ref[...] = _layer_norm(DEEPNORM_ALPHA * x_ref[...] + mix, g_ref[...], b_ref[...])


def _out_proj_ln(x2, o_dsa, o_swa, w_o, g, bta, tm):
    t, d = x2.shape
    row = lambda i: (i, 0)
    const = lambda i: (0, 0)
    return pl.pallas_call(
        _outproj_kernel,
        out_shape=jax.ShapeDtypeStruct((t, d), F32),
        grid=(t // tm,),
        in_specs=[pl.BlockSpec((tm, d), row),
                  pl.BlockSpec((tm, o_dsa.shape[1]), row),
                  pl.BlockSpec((tm, o_swa.shape[1]), row),
                  pl.BlockSpec(w_o.shape, const),
                  pl.BlockSpec((1, d), const),
                  pl.BlockSpec((1, d), const)],
        out_specs=pl.BlockSpec((tm, d), row),
        compiler_params=_cparams(1),
        name="out_proj_ln",
    )(x2, o_dsa, o_swa, w_o, g, bta)


def _memkv_kernel(mem_ref, wk_ref, wv_ref, k_ref, v_ref):
    m = mem_ref[...].astype(BF16)
    k_ref[...] = jnp.dot(m, wk_ref[...], preferred_element_type=F32).astype(k_ref.dtype)
    v_ref[...] = jnp.dot(m, wv_ref[...], preferred_element_type=F32).astype(v_ref.dtype)


def _mem_kv(mem2, wk, wv, tm):
    t, d = mem2.shape
    row = lambda i: (i, 0)
    const = lambda i: (0, 0)
    return pl.pallas_call(
        _memkv_kernel,
        out_shape=(jax.ShapeDtypeStruct((t, d), BF16), jax.ShapeDtypeStruct((t, d), BF16)),
        grid=(t // tm,),
        in_specs=[pl.BlockSpec((tm, d), row), pl.BlockSpec(wk.shape, const), pl.BlockSpec(wv.shape, const)],
        out_specs=(pl.BlockSpec((tm, d), row), pl.BlockSpec((tm, d), row)),
        compiler_params=_cparams(1),
        name="mem_kv",
    )(mem2, wk, wv)


def _cross_kernel(h_ref, wq_ref, km_ref, vm_ref, wo_ref, g_ref, b_ref, o_ref):
    hf = h_ref[0]
    d = hf.shape[1]
    hd = d // MEM_HEADS
    q = (jnp.dot(hf.astype(BF16), wq_ref[...], preferred_element_type=F32) * (hd ** -0.5)).astype(BF16)
    outs = []
    for hh in range(MEM_HEADS):
        cols = slice(hh * hd, (hh + 1) * hd)
        s = lax.dot_general(q[:, cols], km_ref[0, :, cols], NT_DIMS, preferred_element_type=F32)
        m = jnp.max(s, axis=-1, keepdims=True)
        p = jnp.exp(s - m)
        den = jnp.sum(p, axis=-1, keepdims=True)
        o = jnp.dot(p.astype(BF16), vm_ref[0, :, cols], preferred_element_type=F32) / den
        outs.append(o.astype(BF16))
    o_all = jnp.concatenate(outs, axis=-1)
    c = jnp.dot(o_all, wo_ref[...], preferred_element_type=F32)
    o_ref[0] = _layer_norm(DEEPNORM_ALPHA * hf + c, g_ref[...], b_ref[...])


def _cross_ln(h3, wq, km, vm, wo, g, bta, tm):
    b, s, d = h3.shape
    m = km.shape[1]
    row = lambda bi, i: (bi, i, 0)
    per_b = lambda bi, i: (bi, 0, 0)
    const = lambda bi, i: (0, 0)
    return pl.pallas_call(
        _cross_kernel,
        out_shape=jax.ShapeDtypeStruct((b, s, d), F32),
        grid=(b, s // tm),
        in_specs=[pl.BlockSpec((1, tm, d), row),
                  pl.BlockSpec(wq.shape, const),
                  pl.BlockSpec((1, m, d), per_b),
                  pl.BlockSpec((1, m, d), per_b),
                  pl.BlockSpec(wo.shape, const),
                  pl.BlockSpec((1, d), const),
                  pl.BlockSpec((1, d), const)],
        out_specs=pl.BlockSpec((1, tm, d), row),
        compiler_params=_cparams(2),
        name="cross_ln",
    )(h3, wq, km, vm, wo, g, bta)


def _mlp_kernel(h_ref, wu_ref, wd_ref, g_ref, b_ref, o_ref, *, f_chunk):
    hf = h_ref[...]
    hb = hf.astype(BF16)
    acc = DEEPNORM_ALPHA * hf
    for c0 in range(0, wu_ref.shape[1], f_chunk):
        u = jnp.maximum(jnp.dot(hb, wu_ref[:, c0:c0 + f_chunk], preferred_element_type=F32), 0.0)
        acc = acc + jnp.dot((u * u).astype(BF16), wd_ref[c0:c0 + f_chunk, :], preferred_element_type=F32)
    o_ref[...] = _layer_norm(acc, g_ref[...], b_ref[...])


def _mlp_ln(h2, w_up, w_down, g, bta, tm):
    t, d = h2.shape
    row = lambda i: (i, 0)
    const = lambda i: (0, 0)
    once = pl.Buffered(1)
    return pl.pallas_call(
        functools.partial(_mlp_kernel, f_chunk=1024),
        out_shape=jax.ShapeDtypeStruct((t, d), F32),
        grid=(t // tm,),
        in_specs=[pl.BlockSpec((tm, d), row),
                  pl.BlockSpec(w_up.shape, const, pipeline_mode=once),
                  pl.BlockSpec(w_down.shape, const, pipeline_mode=once),
                  pl.BlockSpec((1, d), const),
                  pl.BlockSpec((1, d), const)],
        out_specs=pl.BlockSpec((tm, d), row),
        compiler_params=_cparams(1),
        name="mlp_ln",
    )(h2, w_up, w_down, g, bta)


def _tile(n, pref):
    while n % pref:
        pref //= 2
    return pref


def kernel(x, mem, positions, w_in, b_in, swa_sinks, w_o, ln1_g, ln1_b, wq_mem, wk_mem, wv_mem, wo_mem,
           ln2_g, ln2_b, w_up, w_down, ln3_g, ln3_b):
    b, s, d = x.shape
    t = b * s
    assert w_in.shape[0] == DEPTH and s % LANES == 0
    n_sel = min(TOPK_MAX, s // 4)

    inv = ROPE_THETA ** (-jnp.arange(0, HEAD_DIM, 2, dtype=F32) / HEAD_DIM)
    inv_row = jnp.tile(inv, LANES // inv.shape[0])[None, :]
    sgn_row = jnp.where((jnp.arange(LANES) % HEAD_DIM) < HEAD_DIM // 2, -1.0, 1.0).astype(F32)[None, :]
    sizes = (DSA_HEADS * HEAD_DIM, HEAD_DIM, HEAD_DIM, IDX_HEADS * IDX_DIM, IDX_DIM, IDX_HEADS,
             SWA_HEADS * HEAD_DIM, SWA_KV_HEADS * HEAD_DIM, SWA_KV_HEADS * HEAD_DIM)
    off = np.concatenate([[0], np.cumsum(sizes)])
    order = (0, 3, 6, 7, 1, 4, 8, 2, 5)
    pad = C_END - int(off[-1])
    w_p = jnp.concatenate([w_in[0][:, off[k]:off[k + 1]] for k in order]
                          + [jnp.zeros((d, pad), w_in.dtype)], axis=1).astype(BF16)
    b_p = jnp.concatenate([b_in[0][off[k]:off[k + 1]] for k in order] + [jnp.zeros((pad,), b_in.dtype)])[None, :]

    x2 = x.reshape(t, d)
    cos, sin = _rope_tables(positions.reshape(t, 1), inv_row, sgn_row, _tile(t, 2048))
    tq = _tile(s, 256)
    tk = _tile(s, 2 * tq)
    nt = s // tk
    qd, qi, qs, sk, kk, sv, dvt, iwt3 = _in_proj(x2, w_p, b_p, cos, sin, tk)

    dvt4 = dvt.reshape(b, nt, V_ROWS, tk)
    iwt = jnp.swapaxes(iwt3.reshape(b, nt, IDX_HEADS, tk), 1, 2).reshape(b, IDX_HEADS, s)
    o_dsa = _dsa(qi.reshape(b, s, -1), qd.reshape(b, s, -1), kk.reshape(b, nt, tk, LANES), dvt4, iwt, tq, n_sel)

    o_swa = _swa(swa_sinks[0], qs.reshape(b, s, -1), sk.reshape(b, s, LANES), sv.reshape(b, s, LANES),
                 _tile(s, 512))

    h1 = _out_proj_ln(x2, o_dsa.reshape(t, -1), o_swa.reshape(t, -1), w_o[0].astype(BF16),
                      ln1_g[0][None, :], ln1_b[0][None, :], _tile(t, 1024))

    mt = mem.shape[1]
    km, vm = _mem_kv(mem.reshape(b * mt, d), wk_mem[0].astype(BF16), wv_mem[0].astype(BF16), _tile(b * mt, 256))
    h2 = _cross_ln(h1.reshape(b, s, d), wq_mem[0].astype(BF16), km.reshape(b, mt, d), vm.reshape(b, mt, d),
                   wo_mem[0].astype(BF16), ln2_g[0][None, :], ln2_b[0][None, :], _tile(s, 512))

    h3 = _mlp_ln(h2.reshape(t, d), w_up[0].astype(BF16), w_down[0].astype(BF16),
                 ln3_g[0][None, :], ln3_b[0][None, :], _tile(t, 512))
    return h3.reshape(b, s, d)
```
